```python
import math
import jax, jax.numpy as jnp
from jax import lax
import numpy as np

D_MODEL = 1024
BATCH = 2
SEQ = 16384
DEPTH = 2

N_MIXERS = 2
EPS = 1e-6
PLE_DIM = 256
SSD_EXPAND = 2
SSD_D_INNER = SSD_EXPAND * D_MODEL
SSD_HEAD_DIM = 64
SSD_N_HEADS = SSD_D_INNER // SSD_HEAD_DIM
SSD_N_GROUPS = 4
SSD_D_STATE = 128
SSD_CONV = 4
SSD_CHUNK = 128
SSD_CONV_DIM = SSD_D_INNER + 2 * SSD_N_GROUPS * SSD_D_STATE
SSD_IN_DIM = SSD_D_INNER + SSD_CONV_DIM + SSD_N_HEADS
SB_HEAD_DIM = 128
SB_N_HEADS = D_MODEL // SB_HEAD_DIM
SB_Q_BLOCK = 128
MOE_GROUPS = 8
MOE_EXPERTS_PER_GROUP = 8
MOE_EXPERTS = MOE_GROUPS * MOE_EXPERTS_PER_GROUP
MOE_TOP_K = 2
MOE_D_FF = 512
MOE_ROW_BLOCK = 128

N_SSD_LAYERS = (DEPTH + N_MIXERS - 1) // N_MIXERS
N_SB_LAYERS = DEPTH // N_MIXERS

kernel_name = "hybrid_ssd_stickbreaking_hmoe"


def rms_norm(x, g):
    xf = x.astype(jnp.float32)
    y = xf * lax.rsqrt(jnp.mean(xf * xf, axis=-1, keepdims=True) + EPS)
    return (y * g.astype(jnp.float32)).astype(x.dtype)


def causal_depthwise_conv(x, w, b):
    C = x.shape[-1]
    y = lax.conv_general_dilated(
        x, w[:, None, :].astype(x.dtype), window_strides=(1,),
        padding=[(SSD_CONV - 1, 0)], dimension_numbers=('NWC', 'WIO', 'NWC'),
        feature_group_count=C)
    return y + b.astype(x.dtype)


def ssd_chunked_scan(x, dt, A, Bm, Cm):
    Bsz, L, H, P = x.shape
    G, N, Q = SSD_N_GROUPS, SSD_D_STATE, SSD_CHUNK
    R = H // G
    nc = L // Q
    xc = (x * dt[..., None]).reshape(Bsz, nc, Q, G, R, P)
    acum = jnp.cumsum((dt * A).reshape(Bsz, nc, Q, H), axis=2)
    Bc = Bm.reshape(Bsz, nc, Q, G, N)
    Cc = Cm.reshape(Bsz, nc, Q, G, N)
    causal = jnp.tril(jnp.ones((Q, Q), dtype=bool))
    seg = acum[:, :, :, None, :] - acum[:, :, None, :, :]
    decay = jnp.exp(jnp.where(causal[None, None, :, :, None], seg, -jnp.inf))
    decay = decay.reshape(Bsz, nc, Q, Q, G, R)
    cb = jnp.einsum('bclgn,bcsgn->bclsg', Cc, Bc)
    w_ls = cb[..., None] * decay
    y_diag = jnp.einsum('bclsgr,bcsgrp->bclgrp', w_ls, xc)
    decay_to_end = jnp.exp(acum[:, :, -1:, :] - acum).reshape(Bsz, nc, Q, G, R)
    states = jnp.einsum('bcsgn,bcsgrp->bcgrpn', Bc, xc * decay_to_end[..., None])
    chunk_decay = jnp.exp(acum[:, :, -1, :]).reshape(Bsz, nc, G, R)

    def step(h, inp):
        s_c, a_c = inp
        return h * a_c[..., None, None] + s_c, h

    h0 = jnp.zeros((Bsz, G, R, P, N), states.dtype)
    _, states_in = lax.scan(step, h0, (jnp.swapaxes(states, 0, 1), jnp.swapaxes(chunk_decay, 0, 1)))
    states_in = jnp.swapaxes(states_in, 0, 1)
    decay_in = jnp.exp(acum).reshape(Bsz, nc, Q, G, R)
    y_off = jnp.einsum('bclgn,bcgrpn->bclgrp', Cc, states_in) * decay_in[..., None]
    return (y_diag + y_off).reshape(Bsz, L, H, P)


def ssd_mixer(u, w_in, conv_w, conv_b, dt_bias, a_log, d_skip, gnorm, w_out):
    Bsz, L, _ = u.shape
    zxbcdt = u @ w_in
    z = zxbcdt[..., :SSD_D_INNER]
    xbc = zxbcdt[..., SSD_D_INNER:SSD_D_INNER + SSD_CONV_DIM]
    dt_raw = zxbcdt[..., SSD_D_INNER + SSD_CONV_DIM:]
    xbc = jax.nn.silu(causal_depthwise_conv(xbc, conv_w, conv_b))
    gn = SSD_N_GROUPS * SSD_D_STATE
    xs = xbc[..., :SSD_D_INNER].reshape(Bsz, L, SSD_N_HEADS, SSD_HEAD_DIM)
    Bm = xbc[..., SSD_D_INNER:SSD_D_INNER + gn].reshape(Bsz, L, SSD_N_GROUPS, SSD_D_STATE)
    Cm = xbc[..., SSD_D_INNER + gn:].reshape(Bsz, L, SSD_N_GROUPS, SSD_D_STATE)
    dt = jax.nn.softplus(dt_raw.astype(jnp.float32) + dt_bias.astype(jnp.float32))
    A = -jnp.exp(a_log.astype(jnp.float32))
    y = ssd_chunked_scan(xs, dt, A, Bm, Cm)
    y = y + xs * d_skip[:, None]
    yz = (y.reshape(Bsz, L, SSD_D_INNER) * jax.nn.silu(z)).astype(jnp.float32)
    yg = yz.reshape(Bsz, L, SSD_N_GROUPS, SSD_D_INNER // SSD_N_GROUPS)
    yg = yg * lax.rsqrt(jnp.mean(yg * yg, axis=-1, keepdims=True) + EPS)
    yn = (yg.reshape(Bsz, L, SSD_D_INNER) * gnorm.astype(jnp.float32)).astype(u.dtype)
    return yn @ w_out


def stick_breaking_attention(u, w_qkv, w_o):
    Bsz, L, _ = u.shape
    Qb = SB_Q_BLOCK
    qkv = (u @ w_qkv).reshape(Bsz, L, 3, SB_N_HEADS, SB_HEAD_DIM)
    q = qkv[:, :, 0].transpose(0, 2, 1, 3) * (1.0 / math.sqrt(SB_HEAD_DIM))
    k = qkv[:, :, 1].transpose(0, 2, 1, 3)
    v = qkv[:, :, 2].transpose(0, 2, 1, 3)
    nb = L // Qb
    pos = jnp.arange(Qb)
    tri = (pos[:, None] >= pos[None, :]).astype(jnp.float32)
    diag_before = pos[None, :] < pos[:, None]
    outs = []
    for bi in range(nb):
        nk = bi + 1
        Lk = nk * Qb
        q_blk = q[:, :, bi * Qb:(bi + 1) * Qb]
        z = jnp.einsum('bhqd,bhkd->bhqk', q_blk, k[:, :, :Lk]).astype(jnp.float32)
        before = jnp.concatenate([jnp.ones((Qb, bi * Qb), bool), diag_before], axis=1)
        log_keep = jnp.where(before, -jax.nn.softplus(z), 0.0).reshape(Bsz, SB_N_HEADS, Qb, nk, Qb)
        within = jnp.einsum('bhqns,sj->bhqnj', log_keep, tri, precision=lax.Precision.HIGHEST)
        later = (jnp.arange(nk)[:, None] > jnp.arange(nk)[None, :]).astype(jnp.float32)
        offset = jnp.einsum('bhqm,mn->bhqn', within[..., 0], later, precision=lax.Precision.HIGHEST)
        log_a = z + (within + offset[..., None]).reshape(Bsz, SB_N_HEADS, Qb, Lk)
        a = jnp.exp(jnp.where(before, log_a, -jnp.inf))
        outs.append(jnp.einsum('bhqk,bhkd->bhqd', a.astype(v.dtype), v[:, :, :Lk]))
    o = jnp.concatenate(outs, axis=2)
    o = o.transpose(0, 2, 1, 3).reshape(Bsz, L, SB_N_HEADS * SB_HEAD_DIM)
    return o @ w_o


def hier_moe(h, w_rg, b_rg, w_re, b_re, w_gate, w_up, w_down):
    Bsz, L, D = h.shape
    T = Bsz * L
    ht = h.reshape(T, D)
    g_logits = (ht @ w_rg + b_rg).astype(jnp.float32)
    g_prob = jax.nn.softmax(g_logits, axis=-1)
    g_top, g_idx = lax.top_k(g_logits, 1)
    g_w = jnp.take_along_axis(g_prob, g_idx, axis=1)[:, 0]
    g_idx = g_idx[:, 0]
    e_logits = (ht @ w_re + b_re).astype(jnp.float32).reshape(T, MOE_GROUPS, MOE_EXPERTS_PER_GROUP)
    e_sel = jnp.take_along_axis(e_logits, g_idx[:, None, None], axis=1)[:, 0]
    e_top, e_idx = lax.top_k(e_sel, MOE_TOP_K)
    gates = g_w[:, None] * jax.nn.softmax(e_top, axis=-1)
    expert = g_idx[:, None] * MOE_EXPERTS_PER_GROUP + e_idx
    TK = T * MOE_TOP_K
    flat_e = expert.reshape(TK)
    flat_g = gates.reshape(TK)
    flat_tok = jnp.arange(TK) // MOE_TOP_K
    order = jnp.argsort(flat_e, stable=True)
    se, stok, sg = flat_e[order], flat_tok[order], flat_g[order]
    counts = jnp.bincount(flat_e, length=MOE_EXPERTS)
    padded = ((counts + MOE_ROW_BLOCK - 1) // MOE_ROW_BLOCK) * MOE_ROW_BLOCK
    pad_end = jnp.cumsum(padded)
    pad_start = pad_end - padded
    start = jnp.cumsum(counts) - counts
    dest = pad_start[se] + jnp.arange(TK) - start[se]
    n_rows = -(-TK // MOE_ROW_BLOCK) * MOE_ROW_BLOCK + MOE_EXPERTS * MOE_ROW_BLOCK
    n_blk = n_rows // MOE_ROW_BLOCK
    row_tok = jnp.full((n_rows,), T, jnp.int32).at[dest].set(stok.astype(jnp.int32))
    row_gate = jnp.zeros((n_rows,), jnp.float32).at[dest].set(sg)
    blk_expert = jnp.minimum(
        jnp.searchsorted(pad_end, jnp.arange(n_blk) * MOE_ROW_BLOCK, side='right'),
        MOE_EXPERTS - 1)
    h_pad = jnp.concatenate([ht, jnp.zeros((1, D), ht.dtype)], axis=0)
    x_rows = h_pad[row_tok].reshape(n_blk, MOE_ROW_BLOCK, D)

    def expert_block(args):
        xb, e = args
        return (jax.nn.silu(xb @ w_gate[e]) * (xb @ w_up[e])) @ w_down[e]

    y_rows = lax.map(expert_block, (x_rows, blk_expert)).reshape(n_rows, D)
    y_rows = y_rows * row_gate[:, None].astype(y_rows.dtype)
    out = jnp.zeros((T + 1, D), y_rows.dtype).at[row_tok].add(y_rows)[:T]
    return out.reshape(Bsz, L, D).astype(h.dtype)


def setup_inputs(seed: int = 0) -> dict:
    key = jax.random.key(seed)
    ks = iter(jax.random.split(key, 40))

    def nrm(shape, scale):
        return jax.random.normal(next(ks), shape, jnp.float32) * scale

    ns, nb = N_SSD_LAYERS, N_SB_LAYERS
    x = nrm((BATCH, SEQ, D_MODEL), 1.0)
    p = nrm((DEPTH, BATCH, SEQ, PLE_DIM), 1.0)
    ssd_norm = 1.0 + nrm((ns, D_MODEL), 0.02)
    ssd_w_in = nrm((ns, D_MODEL, SSD_IN_DIM), D_MODEL ** -0.5)
    ssd_conv_w = nrm((ns, SSD_CONV, SSD_CONV_DIM), SSD_CONV ** -0.5)
    ssd_conv_b = nrm((ns, SSD_CONV_DIM), 0.01)
    dt0 = jnp.exp(jax.random.uniform(next(ks), (ns, SSD_N_HEADS), jnp.float32,
                                     minval=math.log(1e-3), maxval=math.log(1e-1)))
    ssd_dt_bias = dt0 + jnp.log(-jnp.expm1(-dt0))
    ssd_a_log = jnp.log(jax.random.uniform(next(ks), (ns, SSD_N_HEADS), jnp.float32, minval=1.0, maxval=16.0))
    ssd_d = 1.0 + nrm((ns, SSD_N_HEADS), 0.1)
    ssd_gnorm = 1.0 + nrm((ns, SSD_D_INNER), 0.02)
    ssd_w_out = nrm((ns, SSD_D_INNER, D_MODEL), SSD_D_INNER ** -0.5)
    sb_norm = 1.0 + nrm((nb, D_MODEL), 0.02)
    sb_w_qkv = nrm((nb, D_MODEL, 3 * SB_N_HEADS * SB_HEAD_DIM), D_MODEL ** -0.5)
    sb_w_o = nrm((nb, SB_N_HEADS * SB_HEAD_DIM, D_MODEL), D_MODEL ** -0.5)
    moe_norm = 1.0 + nrm((DEPTH, D_MODEL), 0.02)
    moe_w_rg = nrm((DEPTH, D_MODEL, MOE_GROUPS), D_MODEL ** -0.5)
    moe_b_rg = nrm((DEPTH, MOE_GROUPS), 0.01)
    moe_w_re = nrm((DEPTH, D_MODEL, MOE_EXPERTS), D_MODEL ** -0.5)
    moe_b_re = nrm((DEPTH, MOE_EXPERTS), 0.01)
    moe_w_gate = nrm((DEPTH, MOE_EXPERTS, D_MODEL, MOE_D_FF), D_MODEL ** -0.5)
    moe_w_up = nrm((DEPTH, MOE_EXPERTS, D_MODEL, MOE_D_FF), D_MODEL ** -0.5)
    moe_w_down = nrm((DEPTH, MOE_EXPERTS, MOE_D_FF, D_MODEL), MOE_D_FF ** -0.5)
    ple_norm = 1.0 + nrm((DEPTH, D_MODEL), 0.02)
    ple_w_gate = nrm((DEPTH, D_MODEL, D_MODEL), D_MODEL ** -0.5)
    ple_b_gate = nrm((DEPTH, D_MODEL), 0.01)
    ple_w_proj = nrm((DEPTH, PLE_DIM, D_MODEL), PLE_DIM ** -0.5)
    final_norm = 1.0 + nrm((D_MODEL,), 0.02)
    return {"x": x, "p": p,
            "ssd_norm": ssd_norm, "ssd_w_in": ssd_w_in, "ssd_conv_w": ssd_conv_w, "ssd_conv_b": ssd_conv_b,
            "ssd_dt_bias": ssd_dt_bias, "ssd_a_log": ssd_a_log, "ssd_d": ssd_d, "ssd_gnorm": ssd_gnorm,
            "ssd_w_out": ssd_w_out,
            "sb_norm": sb_norm, "sb_w_qkv": sb_w_qkv, "sb_w_o": sb_w_o,
            "moe_norm": moe_norm, "moe_w_rg": moe_w_rg, "moe_b_rg": moe_b_rg, "moe_w_re": moe_w_re,
            "moe_b_re": moe_b_re, "moe_w_gate": moe_w_gate, "moe_w_up": moe_w_up, "moe_w_down": moe_w_down,
            "ple_norm": ple_norm, "ple_w_gate": ple_w_gate, "ple_b_gate": ple_b_gate, "ple_w_proj": ple_w_proj,
            "final_norm": final_norm}


def reference(x, p, ssd_norm, ssd_w_in, ssd_conv_w, ssd_conv_b, ssd_dt_bias, ssd_a_log, ssd_d, ssd_gnorm,
              ssd_w_out, sb_norm, sb_w_qkv, sb_w_o, moe_norm, moe_w_rg, moe_b_rg, moe_w_re, moe_b_re,
              moe_w_gate, moe_w_up, moe_w_down, ple_norm, ple_w_gate, ple_b_gate, ple_w_proj, final_norm):
    h = x
    for i in range(DEPTH):
        j = i // N_MIXERS
        if i % N_MIXERS == 0:
            h = h + ssd_mixer(rms_norm(h, ssd_norm[j]), ssd_w_in[j], ssd_conv_w[j], ssd_conv_b[j],
                              ssd_dt_bias[j], ssd_a_log[j], ssd_d[j], ssd_gnorm[j], ssd_w_out[j])
        else:
            h = h + stick_breaking_attention(rms_norm(h, sb_norm[j]), sb_w_qkv[j], sb_w_o[j])
        h = h + hier_moe(rms_norm(h, moe_norm[i]), moe_w_rg[i], moe_b_rg[i], moe_w_re[i], moe_b_re[i],
                         moe_w_gate[i], moe_w_up[i], moe_w_down[i])
        gate = jax.nn.sigmoid((rms_norm(h, ple_norm[i]) @ ple_w_gate[i] + ple_b_gate[i]).astype(jnp.float32))
        h = h + (gate * (p[i] @ ple_w_proj[i]).astype(jnp.float32)).astype(h.dtype)
    return rms_norm(h, final_norm)
```

```python
import functools
import math

import jax
import jax.numpy as jnp
from jax import lax
from jax.experimental import pallas as pl
from jax.experimental.pallas import tpu as pltpu

F32 = jnp.float32
BF16 = jnp.bfloat16
I32 = jnp.int32

EPS = 1e-6
V7X_LANES = 128
V7X_SUBLANES = 8
V7X_VMEM_LIMIT_BYTES = 56 * 1024 * 1024

SSD_HEAD_DIM = 64
SSD_N_GROUPS = 4
SSD_D_STATE = 128
SSD_CHUNK = 128
SB_HEAD_DIM = 128
MOE_GROUPS = 8
MOE_TOP_K = 2

ATTN_DEAD_LOG = -110.0


def _cparams(n_axes):
    return pltpu.CompilerParams(
        dimension_semantics=("arbitrary",) * n_axes,
        vmem_limit_bytes=V7X_VMEM_LIMIT_BYTES,
    )


def _dot(a, b):
    return jnp.dot(a, b, preferred_element_type=F32)


def _dot_nt(a, b):
    return lax.dot_general(a, b, (((1,), (1,)), ((), ())), preferred_element_type=F32)


def _split3(x):
    hi = x.astype(BF16)
    r1 = x - hi.astype(F32)
    mid = r1.astype(BF16)
    lo = (r1 - mid.astype(F32)).astype(BF16)
    return hi, mid, lo


def _split2(x):
    hi = x.astype(BF16)
    lo = (x - hi.astype(F32)).astype(BF16)
    return hi, lo


def _dot_exact_lhs(a_bf16, x):
    hi, mid, lo = _split3(x)
    return _dot(a_bf16, hi) + _dot(a_bf16, mid) + _dot(a_bf16, lo)


def _sigmoid(x):
    return 1.0 / (1.0 + jnp.exp(-x))


def _softplus(x):
    return jnp.maximum(x, 0.0) + jnp.log(1.0 + jnp.exp(-jnp.abs(x)))


def _rmsnorm(x, g):
    ms = jnp.mean(x * x, axis=-1, keepdims=True)
    return (x * lax.rsqrt(ms + EPS)) * g


def _in_proj_kernel(x_ref, g_ref, wz_ref, wx_ref, wdt_ref, dtb_ref, z_ref, xbc_ref, dt_ref):
    xn = _rmsnorm(x_ref[...], g_ref[...]).astype(BF16)
    z_ref[...] = _dot(xn, wz_ref[...])
    xbc_ref[...] = _dot(xn, wx_ref[...])
    dt_ref[...] = _softplus(_dot(xn, wdt_ref[...]) + dtb_ref[...])


def _in_proj(x2, g, wz, wx, wdt, dtb, *, tm):
    T, D = x2.shape
    nz, nx, nd = wz.shape[1], wx.shape[1], wdt.shape[1]
    full = lambda i: (0, 0)
    return pl.pallas_call(
        _in_proj_kernel,
        grid=(T // tm,),
        in_specs=[
            pl.BlockSpec((tm, D), lambda i: (i, 0)),
            pl.BlockSpec((1, D), full),
            pl.BlockSpec((D, nz), full),
            pl.BlockSpec((D, nx), full),
            pl.BlockSpec((D, nd), full),
            pl.BlockSpec((1, nd), full),
        ],
        out_specs=[
            pl.BlockSpec((tm, nz), lambda i: (i, 0)),
            pl.BlockSpec((tm, nx), lambda i: (i, 0)),
            pl.BlockSpec((tm, nd), lambda i: (i, 0)),
        ],
        out_shape=[
            jax.ShapeDtypeStruct((T, nz), F32),
            jax.ShapeDtypeStruct((T, nx), F32),
            jax.ShapeDtypeStruct((T, nd), F32),
        ],
        compiler_params=_cparams(1),
        name="ssd_in_proj",
    )(x2, g, wz, wx, wdt, dtb)


def _ssd_kernel(xbc_ref, z_ref, dt_ref, cw_ref, cb_ref, a_ref, dsk_ref, gn_ref, yn_ref,
                state_ref, cbuf_ref, act_ref, yz_ref, *, Q, d_inner, n_groups, d_state, head_dim):
    c = pl.program_id(1)
    N = d_state
    gw = d_inner // n_groups
    pairs_per_group = gw // V7X_LANES
    halo = V7X_SUBLANES

    @pl.when(c == 0)
    def _init():
        state_ref[...] = jnp.zeros_like(state_ref)
        cbuf_ref[0:halo, :] = jnp.zeros((halo, cbuf_ref.shape[1]), F32)

    x_in = xbc_ref[...]
    cbuf_ref[halo:halo + Q, :] = x_in
    cw = cw_ref[...]
    conv = (cbuf_ref[halo - 3:halo - 3 + Q, :] * cw[0:1, :]
            + cbuf_ref[halo - 2:halo - 2 + Q, :] * cw[1:2, :]
            + cbuf_ref[halo - 1:halo - 1 + Q, :] * cw[2:3, :]
            + x_in * cw[3:4, :]) + cb_ref[...]
    cbuf_ref[0:halo, :] = x_in[Q - halo:Q, :]
    act_ref[...] = conv * _sigmoid(conv)

    dt = dt_ref[...]
    rows = lax.broadcasted_iota(I32, (Q, Q), 0)
    cols = lax.broadcasted_iota(I32, (Q, Q), 1)
    causal = rows >= cols
    tril = jnp.where(causal, 1.0, 0.0).astype(BF16)
    acum = _dot_exact_lhs(tril, dt * a_ref[...])
    acum_t = acum.T
    a_last = acum[Q - 1:Q, :]
    dte = jnp.exp(a_last - acum)
    din = jnp.exp(acum)
    cdec = jnp.exp(a_last)

    lane = lax.broadcasted_iota(I32, (Q, V7X_LANES), 1)
    lo_half = lane < head_dim
    lo_half_row = lo_half[0:1, :]

    def pair(v, h0, mask):
        return jnp.where(mask, v[:, h0:h0 + 1], v[:, h0 + 1:h0 + 2])

    for g in range(n_groups):
        b_f = act_ref[:, d_inner + g * N:d_inner + (g + 1) * N]
        b_g = b_f.astype(BF16)
        c_g = act_ref[:, d_inner + n_groups * N + g * N:d_inner + n_groups * N + (g + 1) * N].astype(BF16)
        cb = _dot_nt(c_g, b_g)
        s_old = state_ref[:, g * gw:(g + 1) * gw]
        y_off = _dot(c_g, s_old.astype(BF16))
        xw_parts = []
        cd_parts = []
        for jj in range(pairs_per_group):
            j = g * pairs_per_group + jj
            h0 = 2 * j
            sl = slice(j * V7X_LANES, (j + 1) * V7X_LANES)
            xs_p = act_ref[:, sl]
            xdt = xs_p * pair(dt, h0, lo_half)
            xdt_b = xdt.astype(BF16)
            ys = []
            for hh in (h0, h0 + 1):
                seg = acum[:, hh:hh + 1] - acum_t[hh:hh + 1, :]
                dec = jnp.exp(jnp.where(causal, seg, -jnp.inf))
                ys.append(_dot((cb * dec).astype(BF16), xdt_b))
            y = jnp.where(lo_half, ys[0], ys[1])
            y = y + y_off[:, jj * V7X_LANES:(jj + 1) * V7X_LANES] * pair(din, h0, lo_half)
            y = y + xs_p * dsk_ref[:, sl]
            zp = z_ref[:, sl]
            yz_ref[:, sl] = y * (zp * _sigmoid(zp))
            xw_parts.append((xdt * pair(dte, h0, lo_half)).astype(BF16))
            cd_parts.append(pair(cdec, h0, lo_half_row))
        xw_g = jnp.concatenate(xw_parts, axis=1)
        new_states = _dot(b_f.T.astype(BF16), xw_g)
        state_ref[:, g * gw:(g + 1) * gw] = s_old * jnp.concatenate(cd_parts, axis=1) + new_states
        yzg = yz_ref[:, g * gw:(g + 1) * gw]
        ms = jnp.mean(yzg * yzg, axis=-1, keepdims=True)
        yn_ref[:, g * gw:(g + 1) * gw] = (
            (yzg * lax.rsqrt(ms + EPS)) * gn_ref[:, g * gw:(g + 1) * gw]).astype(yn_ref.dtype)


def _ssd_scan(xbc, z, dt, cw, cb, a_row, dsk, gn, *, d_inner):
    B, L, cdim = xbc.shape
    Q = SSD_CHUNK
    nd = dt.shape[-1]
    full2 = lambda b, c: (0, 0)
    kern = functools.partial(_ssd_kernel, Q=Q, d_inner=d_inner, n_groups=SSD_N_GROUPS,
                             d_state=SSD_D_STATE, head_dim=SSD_HEAD_DIM)
    return pl.pallas_call(
        kern,
        grid=(B, L // Q),
        in_specs=[
            pl.BlockSpec((None, Q, cdim), lambda b, c: (b, c, 0)),
            pl.BlockSpec((None, Q, d_inner), lambda b, c: (b, c, 0)),
            pl.BlockSpec((None, Q, nd), lambda b, c: (b, c, 0)),
            pl.BlockSpec(cw.shape, full2),
            pl.BlockSpec(cb.shape, full2),
            pl.BlockSpec(a_row.shape, full2),
            pl.BlockSpec(dsk.shape, full2),
            pl.BlockSpec(gn.shape, full2),
        ],
        out_specs=pl.BlockSpec((None, Q, d_inner), lambda b, c: (b, c, 0)),
        out_shape=jax.ShapeDtypeStruct((B, L, d_inner), BF16),
        scratch_shapes=[
            pltpu.VMEM((SSD_D_STATE, d_inner), F32),
            pltpu.VMEM((Q + V7X_SUBLANES, cdim), F32),
            pltpu.VMEM((Q, cdim), F32),
            pltpu.VMEM((Q, d_inner), F32),
        ],
        compiler_params=_cparams(2),
        name="ssd_scan",
    )(xbc, z, dt, cw, cb, a_row, dsk, gn)


def _proj_res_kernel(a_ref, w_ref, r_ref, o_ref):
    o_ref[...] = r_ref[...] + _dot(a_ref[...], w_ref[...])


def _proj_res(a, w, res, *, tm):
    T, K = a.shape
    N = w.shape[1]
    return pl.pallas_call(
        _proj_res_kernel,
        grid=(T // tm,),
        in_specs=[
            pl.BlockSpec((tm, K), lambda i: (i, 0)),
            pl.BlockSpec((K, N), lambda i: (0, 0)),
            pl.BlockSpec((tm, N), lambda i: (i, 0)),
        ],
        out_specs=pl.BlockSpec((tm, N), lambda i: (i, 0)),
        out_shape=jax.ShapeDtypeStruct((T, N), F32),
        compiler_params=_cparams(1),
        name="proj_residual",
    )(a, w, res)


def _qkv_kernel(x_ref, g_ref, w_ref, q_ref, k_ref, v_ref, *, n_heads, hd, scale):
    xn = _rmsnorm(x_ref[...], g_ref[...]).astype(BF16)
    qkv = _dot(xn, w_ref[...])
    d = n_heads * hd
    for h in range(n_heads):
        q_ref[h] = (qkv[:, h * hd:(h + 1) * hd] * scale).astype(BF16)
        k_ref[h] = qkv[:, d + h * hd:d + (h + 1) * hd].astype(BF16)
        v_ref[h] = qkv[:, 2 * d + h * hd:2 * d + (h + 1) * hd].astype(BF16)


def _qkv_proj(x, g, w, *, n_heads, hd, tm):
    B, L, D = x.shape
    kern = functools.partial(_qkv_kernel, n_heads=n_heads, hd=hd, scale=1.0 / math.sqrt(hd))
    hspec = pl.BlockSpec((None, n_heads, tm, hd), lambda b, i: (b, 0, i, 0))
    hshape = jax.ShapeDtypeStruct((B, n_heads, L, hd), BF16)
    return pl.pallas_call(
        kern,
        grid=(B, L // tm),
        in_specs=[
            pl.BlockSpec((None, tm, D), lambda b, i: (b, i, 0)),
            pl.BlockSpec((1, D), lambda b, i: (0, 0)),
            pl.BlockSpec(w.shape, lambda b, i: (0, 0)),
        ],
        out_specs=[hspec, hspec, hspec],
        out_shape=[hshape, hshape, hshape],
        compiler_params=_cparams(2),
        name="sb_qkv_proj",
    )(x, g, w)


def _attn_kernel(q_ref, k_ref, v_ref, o_ref, acc_ref, carry_ref, *, tq):
    qi = pl.program_id(1)
    tk = tq
    q = q_ref[...]
    rows = lax.broadcasted_iota(I32, (tq, tk), 0)
    cols = lax.broadcasted_iota(I32, (tq, tk), 1)
    before = cols < rows
    tri = jnp.where(rows >= cols, 1.0, 0.0).astype(BF16)
    tri2 = jnp.concatenate([tri, tri], axis=0)

    def block(kb, diag):
        ks = pl.multiple_of(kb * tk, tk)
        k = k_ref[pl.ds(ks, tk), :]
        v = v_ref[pl.ds(ks, tk), :]
        z = _dot_nt(q, k)
        sp = _softplus(z)
        if diag:
            sp = jnp.where(before, sp, 0.0)
        hi, lo = _split2(sp)
        cs = _dot(jnp.concatenate([hi, lo], axis=1), tri2)
        carry = carry_ref[:, 0:1]
        log_a = z - cs + carry
        if diag:
            log_a = jnp.where(before, log_a, -jnp.inf)
        a = jnp.exp(log_a)
        acc_ref[...] += _dot(a.astype(BF16), v)
        new_carry = carry - cs[:, 0:1]
        carry_ref[...] = jnp.broadcast_to(new_carry, carry_ref.shape)
        return jnp.max(new_carry)

    acc_ref[...] = jnp.zeros_like(acc_ref)
    carry_ref[...] = jnp.zeros_like(carry_ref)
    live0 = block(qi, True)

    def cond(st):
        kb, live = st
        return jnp.logical_and(kb >= 0, live > ATTN_DEAD_LOG)

    def body(st):
        kb, _ = st
        return kb - 1, block(kb, False)

    lax.while_loop(cond, body, (qi - 1, live0))
    o_ref[...] = acc_ref[...].astype(o_ref.dtype)


def _attention(q, k, v, *, B, n_heads, tq):
    BH, L, hd = q.shape
    kern = functools.partial(_attn_kernel, tq=tq)
    return pl.pallas_call(
        kern,
        grid=(BH, L // tq),
        in_specs=[
            pl.BlockSpec((None, tq, hd), lambda bh, i: (bh, i, 0)),
            pl.BlockSpec((None, L, hd), lambda bh, i: (bh, 0, 0)),
            pl.BlockSpec((None, L, hd), lambda bh, i: (bh, 0, 0)),
        ],
        out_specs=pl.BlockSpec((None, tq, hd), lambda bh, i: (bh // n_heads, i, bh % n_heads)),
        out_shape=jax.ShapeDtypeStruct((B, L, n_heads * hd), BF16),
        scratch_shapes=[pltpu.VMEM((tq, hd), F32), pltpu.VMEM((tq, V7X_LANES), F32)],
        compiler_params=_cparams(2),
        name="sb_attention",
    )(q, k, v)


RI_E0, RI_E1, RI_G0, RI_G1, RI_R0, RI_R1 = 0, 1, 2, 3, 4, 5


def _router_kernel(h_ref, g_ref, w_ref, b_ref, ri_ref, cnt_ref, carry_ref, *, n_groups, n_experts):
    i = pl.program_id(0)
    tm = h_ref.shape[0]
    epg = n_experts // n_groups

    @pl.when(i == 0)
    def _init():
        carry_ref[...] = jnp.zeros_like(carry_ref)

    xn = _rmsnorm(h_ref[...], g_ref[...])
    xh, xm, xl = _split3(xn)
    w = w_ref[...]
    wh, wm, wl = _split3(w)
    logits = (_dot(xh, wh) + (_dot(xh, wm) + _dot(xm, wh))
              + (_dot(xm, wm) + _dot(xh, wl) + _dot(xl, wh))) + b_ref[...]

    lane = lax.broadcasted_iota(I32, (tm, V7X_LANES), 1).astype(F32)
    neg = -jnp.inf
    no_lane = float(V7X_LANES)
    gl = jnp.where(lane < n_groups, logits, neg)
    gmax = jnp.max(gl, axis=-1, keepdims=True)
    gidx = jnp.min(jnp.where(gl == gmax, lane, no_lane), axis=-1, keepdims=True)
    g_w = 1.0 / jnp.sum(jnp.exp(gl - gmax), axis=-1, keepdims=True)

    first = n_groups + gidx * epg
    in_group = jnp.logical_and(lane >= first, lane < first + epg)
    el = jnp.where(in_group, logits, neg)
    m0 = jnp.max(el, axis=-1, keepdims=True)
    i0 = jnp.min(jnp.where(el == m0, lane, no_lane), axis=-1, keepdims=True)
    el1 = jnp.where(lane == i0, neg, el)
    m1 = jnp.max(el1, axis=-1, keepdims=True)
    i1 = jnp.min(jnp.where(el1 == m1, lane, no_lane), axis=-1, keepdims=True)
    d = jnp.exp(m1 - m0)
    p0 = 1.0 / (1.0 + d)
    gate0 = g_w * p0
    gate1 = g_w * (d * p0)

    oh0 = jnp.where(lane == i0, 1.0, 0.0)
    oh1 = jnp.where(lane == i1, 1.0, 0.0)
    r = lax.broadcasted_iota(I32, (tm, tm), 0)
    cc = lax.broadcasted_iota(I32, (tm, tm), 1)
    strict = jnp.where(cc < r, 1.0, 0.0).astype(BF16)
    pre0 = _dot(strict, oh0.astype(BF16))
    pre1 = _dot(strict, oh1.astype(BF16))
    cnt0 = jnp.sum(oh0, axis=0, keepdims=True)
    cnt1 = jnp.sum(oh1, axis=0, keepdims=True)
    base = carry_ref[0:1, :]
    rank0 = jnp.sum(oh0 * (pre0 + base), axis=-1, keepdims=True)
    rank1 = jnp.sum(oh1 * (pre1 + (base + cnt0)), axis=-1, keepdims=True)
    total = base + cnt0 + cnt1
    carry_ref[...] = jnp.broadcast_to(total, carry_ref.shape)
    cnt_ref[...] = jnp.broadcast_to(total, cnt_ref.shape)

    e0 = i0 - n_groups
    e1 = i1 - n_groups
    ri = jnp.where(lane == RI_E0, e0, 0.0)
    ri = jnp.where(lane == RI_E1, e1, ri)
    ri = jnp.where(lane == RI_G0, gate0, ri)
    ri = jnp.where(lane == RI_G1, gate1, ri)
    ri = jnp.where(lane == RI_R0, rank0, ri)
    ri = jnp.where(lane == RI_R1, rank1, ri)
    ri_ref[...] = ri


def _router(h2, g, w_r, b_r, *, n_experts, tm):
    T, D = h2.shape
    kern = functools.partial(_router_kernel, n_groups=MOE_GROUPS, n_experts=n_experts)
    return pl.pallas_call(
        kern,
        grid=(T // tm,),
        in_specs=[
            pl.BlockSpec((tm, D), lambda i: (i, 0)),
            pl.BlockSpec((1, D), lambda i: (0, 0)),
            pl.BlockSpec((D, V7X_LANES), lambda i: (0, 0)),
            pl.BlockSpec((1, V7X_LANES), lambda i: (0, 0)),
        ],
        out_specs=[
            pl.BlockSpec((tm, V7X_LANES), lambda i: (i, 0)),
            pl.BlockSpec((V7X_SUBLANES, V7X_LANES), lambda i: (0, 0)),
        ],
        out_shape=[
            jax.ShapeDtypeStruct((T, V7X_LANES), F32),
            jax.ShapeDtypeStruct((V7X_SUBLANES, V7X_LANES), F32),
        ],
        scratch_shapes=[pltpu.VMEM((V7X_SUBLANES, V7X_LANES), F32)],
        compiler_params=_cparams(1),
        name="moe_router",
    )(h2, g, w_r, b_r)


def _row_copy(src_ref, src_row, dst_ref, dst_row, sem):
    return pltpu.make_async_copy(src_ref.at[pl.ds(src_row, 1)], dst_ref.at[pl.ds(dst_row, 1)], sem)


def _dispatch_kernel(dest_ref, h_ref, g_ref, init_ref, rows_ref, xn_ref, sem):
    del init_ref
    tm = h_ref.shape[0]
    xn_ref[...] = _rmsnorm(h_ref[...], g_ref[...])

    def issue(t, c):
        for k in range(MOE_TOP_K):
            _row_copy(xn_ref, t, rows_ref, dest_ref[0, 0, MOE_TOP_K * t + k], sem).start()
        return c

    lax.fori_loop(0, tm, issue, 0)

    def drain(t, c):
        for k in range(MOE_TOP_K):
            _row_copy(xn_ref, t, rows_ref, dest_ref[0, 0, MOE_TOP_K * t + k], sem).wait()
        return c

    lax.fori_loop(0, tm, drain, 0)


def _dispatch(dest3, h2, g, n_rows, *, tm):
    T, D = h2.shape
    init = jnp.zeros((n_rows, D), F32)
    return pl.pallas_call(
        _dispatch_kernel,
        grid=(T // tm,),
        in_specs=[
            pl.BlockSpec((1, 1, MOE_TOP_K * tm), lambda i: (i, 0, 0), memory_space=pltpu.SMEM),
            pl.BlockSpec((tm, D), lambda i: (i, 0)),
            pl.BlockSpec((1, D), lambda i: (0, 0)),
            pl.BlockSpec(memory_space=pl.ANY),
        ],
        out_specs=pl.BlockSpec(memory_space=pl.ANY),
        out_shape=jax.ShapeDtypeStruct((n_rows, D), F32),
        scratch_shapes=[pltpu.VMEM((tm, D), F32), pltpu.SemaphoreType.DMA(())],
        input_output_aliases={3: 0},
        compiler_params=_cparams(1),
        name="moe_dispatch",
    )(dest3, h2, g, init)


def _expert_kernel(be_ref, nu_ref, x_ref, wg_ref, wu_ref, wd_ref, y_ref):
    i = pl.program_id(0)

    @pl.when(i < nu_ref[0])
    def _live():
        x = x_ref[...].astype(BF16)
        gt = _dot(x, wg_ref[0].astype(BF16))
        up = _dot(x, wu_ref[0].astype(BF16))
        act = (gt * _sigmoid(gt)) * up
        y_ref[...] = _dot(act.astype(BF16), wd_ref[0].astype(BF16))

    @pl.when(i >= nu_ref[0])
    def _dead():
        y_ref[...] = jnp.zeros_like(y_ref)


def _experts(blk_expert, n_used, x_rows, w_gate, w_up, w_down, *, rb):
    n_rows, D = x_rows.shape
    E, _, F = w_gate.shape
    n_blk = n_rows // rb

    def row_map(i, be, nu):
        return (jnp.minimum(i, nu[0] - 1), 0)

    def w_map(i, be, nu):
        return (be[i], 0, 0)

    grid_spec = pltpu.PrefetchScalarGridSpec(
        num_scalar_prefetch=2,
        grid=(n_blk,),
        in_specs=[
            pl.BlockSpec((rb, D), row_map),
            pl.BlockSpec((1, D, F), w_map),
            pl.BlockSpec((1, D, F), w_map),
            pl.BlockSpec((1, F, D), w_map),
        ],
        out_specs=pl.BlockSpec((rb, D), lambda i, be, nu: (i, 0)),
    )
    return pl.pallas_call(
        _expert_kernel,
        grid_spec=grid_spec,
        out_shape=jax.ShapeDtypeStruct((n_rows, D), F32),
        compiler_params=_cparams(1),
        name="moe_experts",
    )(blk_expert, n_used, x_rows, w_gate, w_up, w_down)


def _combine_kernel(dest_ref, h_ref, ri_ref, p_ref, yrows_ref, pn_ref, wg_ref, bg_ref, wp_ref, fn_ref,
                    o_ref, ybuf_ref, sem, *, final_norm):
    tm = h_ref.shape[0]

    def issue(t, c):
        for k in range(MOE_TOP_K):
            _row_copy(yrows_ref, dest_ref[0, 0, MOE_TOP_K * t + k], ybuf_ref.at[k], t, sem).start()
        return c

    lax.fori_loop(0, tm, issue, 0)

    def drain(t, c):
        for k in range(MOE_TOP_K):
            _row_copy(yrows_ref, dest_ref[0, 0, MOE_TOP_K * t + k], ybuf_ref.at[k], t, sem).wait()
        return c

    lax.fori_loop(0, tm, drain, 0)

    ri = ri_ref[...]
    h = h_ref[...] + (ybuf_ref[0] * ri[:, RI_G0:RI_G0 + 1] + ybuf_ref[1] * ri[:, RI_G1:RI_G1 + 1])
    xn = _rmsnorm(h, pn_ref[...]).astype(BF16)
    gate = _sigmoid(_dot(xn, wg_ref[...]) + bg_ref[...])
    h = h + gate * _dot(p_ref[...].astype(BF16), wp_ref[...])
    if final_norm:
        h = _rmsnorm(h, fn_ref[...])
    o_ref[...] = h


def _combine(dest3, h2, ri, p2, y_rows, pn, wg, bg, wp, fnorm, *, tm, final_norm):
    T, D = h2.shape
    P = p2.shape[1]
    kern = functools.partial(_combine_kernel, final_norm=final_norm)
    row = lambda i: (0, 0)
    return pl.pallas_call(
        kern,
        grid=(T // tm,),
        in_specs=[
            pl.BlockSpec((1, 1, MOE_TOP_K * tm), lambda i: (i, 0, 0), memory_space=pltpu.SMEM),
            pl.BlockSpec((tm, D), lambda i: (i, 0)),
            pl.BlockSpec((tm, V7X_LANES), lambda i: (i, 0)),
            pl.BlockSpec((tm, P), lambda i: (i, 0)),
            pl.BlockSpec(memory_space=pl.ANY),
            pl.BlockSpec((1, D), row),
            pl.BlockSpec((D, D), row),
            pl.BlockSpec((1, D), row),
            pl.BlockSpec((P, D), row),
            pl.BlockSpec((1, D), row),
        ],
        out_specs=pl.BlockSpec((tm, D), lambda i: (i, 0)),
        out_shape=jax.ShapeDtypeStruct((T, D), F32),
        scratch_shapes=[pltpu.VMEM((MOE_TOP_K, tm, D), F32), pltpu.SemaphoreType.DMA(())],
        compiler_params=_cparams(1),
        name="moe_combine_ple",
    )(dest3, h2, ri, p2, y_rows, pn, wg, bg, wp, fnorm)


TM_PROJ = 256
TM_RES = 512
TM_TOKEN = 256
EXPERT_ROWS = 256
ATTN_BLOCK = 256


def _row(v):
    return v.reshape(1, -1).astype(F32)


def _moe_ple(h2, p2, moe_norm, w_rg, b_rg, w_re, b_re, w_gate, w_up, w_down,
             ple_norm, ple_w_gate, ple_b_gate, ple_w_proj, final_norm, *, last):
    T, D = h2.shape
    G = w_rg.shape[1]
    E = w_re.shape[1]
    pad = V7X_LANES - G - E
    w_r = jnp.concatenate([w_rg, w_re, jnp.zeros((D, pad), F32)], axis=1)
    b_r = jnp.concatenate([b_rg, b_re, jnp.zeros((pad,), F32)]).reshape(1, V7X_LANES)
    ri, cnt = _router(h2, _row(moe_norm), w_r, b_r, n_experts=E, tm=TM_TOKEN)

    rb = EXPERT_ROWS
    counts = cnt[0, G:G + E].astype(I32)
    padded = ((counts + rb - 1) // rb) * rb
    pad_end = jnp.cumsum(padded)
    pad_start = pad_end - padded
    n_blk = (T * MOE_TOP_K) // rb + E
    n_rows = n_blk * rb
    expert = ri[:, RI_E0:RI_E1 + 1].astype(I32)
    rank = ri[:, RI_R0:RI_R1 + 1].astype(I32)
    dest = pad_start[expert] + rank
    dest3 = dest.reshape(T // TM_TOKEN, 1, MOE_TOP_K * TM_TOKEN)
    blk_expert = jnp.minimum(
        jnp.searchsorted(pad_end, jnp.arange(n_blk, dtype=I32) * rb, side="right"), E - 1).astype(I32)
    n_used = (pad_end[-1:] // rb).astype(I32)

    x_rows = _dispatch(dest3, h2, _row(moe_norm), n_rows, tm=TM_TOKEN)
    y_rows = _experts(blk_expert, n_used, x_rows, w_gate, w_up, w_down, rb=rb)
    return _combine(dest3, h2, ri, p2, y_rows, _row(ple_norm), ple_w_gate.astype(BF16),
                    _row(ple_b_gate), ple_w_proj.astype(BF16), _row(final_norm),
                    tm=TM_TOKEN, final_norm=last)


def _ssd_layer(h2, B, L, norm, w_in, conv_w, conv_b, dt_bias, a_log, d_skip, gnorm, w_out):
    T, D = h2.shape
    H = a_log.shape[0]
    d_inner = H * SSD_HEAD_DIM
    cdim = conv_w.shape[1]
    wz = w_in[:, :d_inner].astype(BF16)
    wx = w_in[:, d_inner:d_inner + cdim].astype(BF16)
    padh = V7X_LANES - H
    wdt = jnp.concatenate([w_in[:, d_inner + cdim:], jnp.zeros((D, padh), F32)], axis=1).astype(BF16)
    dtb = jnp.concatenate([dt_bias, jnp.zeros((padh,), F32)]).reshape(1, V7X_LANES)
    a_row = jnp.concatenate([-jnp.exp(a_log.astype(F32)), jnp.zeros((padh,), F32)]).reshape(1, V7X_LANES)
    dsk = jnp.repeat(d_skip.astype(F32), SSD_HEAD_DIM).reshape(1, d_inner)

    z, xbc, dt = _in_proj(h2, _row(norm), wz, wx, wdt, dtb, tm=TM_PROJ)
    yn = _ssd_scan(xbc.reshape(B, L, cdim), z.reshape(B, L, d_inner), dt.reshape(B, L, V7X_LANES),
                   conv_w.astype(F32), _row(conv_b), a_row, dsk, _row(gnorm), d_inner=d_inner)
    return _proj_res(yn.reshape(T, d_inner), w_out.astype(BF16), h2, tm=TM_RES)


def _sb_layer(h2, B, L, norm, w_qkv, w_o):
    T, D = h2.shape
    hd = SB_HEAD_DIM
    n_heads = w_o.shape[0] // hd
    q, k, v = _qkv_proj(h2.reshape(B, L, D), _row(norm), w_qkv.astype(BF16),
                        n_heads=n_heads, hd=hd, tm=TM_PROJ)
    bh = B * n_heads
    o = _attention(q.reshape(bh, L, hd), k.reshape(bh, L, hd), v.reshape(bh, L, hd),
                   B=B, n_heads=n_heads, tq=ATTN_BLOCK)
    return _proj_res(o.reshape(T, n_heads * hd), w_o.astype(BF16), h2, tm=TM_RES)


def kernel(x, p, ssd_norm, ssd_w_in, ssd_conv_w, ssd_conv_b, ssd_dt_bias, ssd_a_log, ssd_d, ssd_gnorm, ssd_w_out, sb_norm, sb_w_qkv, sb_w_o, moe_norm, moe_w_rg, moe_b_rg, moe_w_re, moe_b_re, moe_w_gate, moe_w_up, moe_w_down, ple_norm, ple_w_gate, ple_b_gate, ple_w_proj, final_norm):
    B, L, D = x.shape
    depth = p.shape[0]
    T = B * L
    n_mixers = 2
    h = x.reshape(T, D)
    for i in range(depth):
        j = i // n_mixers
        if i % n_mixers == 0:
            h = _ssd_layer(h, B, L, ssd_norm[j], ssd_w_in[j], ssd_conv_w[j], ssd_conv_b[j], ssd_dt_bias[j],
                           ssd_a_log[j], ssd_d[j], ssd_gnorm[j], ssd_w_out[j])
        else:
            h = _sb_layer(h, B, L, sb_norm[j], sb_w_qkv[j], sb_w_o[j])
        h = _moe_ple(h, p[i].reshape(T, -1), moe_norm[i], moe_w_rg[i], moe_b_rg[i], moe_w_re[i], moe_b_re[i],
                     moe_w_gate[i], moe_w_up[i], moe_w_down[i], ple_norm[i], ple_w_gate[i], ple_b_gate[i],
                     ple_w_proj[i], final_norm, last=(i == depth - 1))
    return h.reshape(B, L, D)
```

```python
import functools
import math

import jax
import jax.numpy as jnp
from jax import lax
from jax.experimental import pallas as pl
from jax.experimental.pallas import tpu as pltpu

F32 = jnp.float32
BF16 = jnp.bfloat16
I32 = jnp.int32

EPS = 1e-6
V7X_LANES = 128
V7X_SUBLANES = 8
V7X_VMEM_LIMIT_BYTES = 56 * 1024 * 1024

SSD_HEAD_DIM = 64
SSD_N_GROUPS = 4
SSD_D_STATE = 128
SSD_CHUNK = 128
SB_HEAD_DIM = 128
MOE_GROUPS = 8
MOE_TOP_K = 2

DMA_UNROLL = 8

ATTN_DEAD_LOG = -110.0


def _cparams(n_axes):
    return pltpu.CompilerParams(
        dimension_semantics=("arbitrary",) * n_axes,
        vmem_limit_bytes=V7X_VMEM_LIMIT_BYTES,
    )


def _dot(a, b):
    return jnp.dot(a, b, preferred_element_type=F32)


def _dot_nt(a, b):
    return lax.dot_general(a, b, (((1,), (1,)), ((), ())), preferred_element_type=F32)


def _split3(x):
    hi = x.astype(BF16)
    r1 = x - hi.astype(F32)
    mid = r1.astype(BF16)
    lo = (r1 - mid.astype(F32)).astype(BF16)
    return hi, mid, lo


def _split2(x):
    hi = x.astype(BF16)
    lo = (x - hi.astype(F32)).astype(BF16)
    return hi, lo


def _dot_exact_lhs(a_bf16, x):
    hi, mid, lo = _split3(x)
    return _dot(a_bf16, hi) + _dot(a_bf16, mid) + _dot(a_bf16, lo)


def _sigmoid(x):
    return 1.0 / (1.0 + jnp.exp(-x))


def _softplus(x):
    return jnp.maximum(x, 0.0) + jnp.log(1.0 + jnp.exp(-jnp.abs(x)))


def _rmsnorm(x, g):
    ms = jnp.mean(x * x, axis=-1, keepdims=True)
    return (x * lax.rsqrt(ms + EPS)) * g


def _in_proj_kernel(x_ref, g_ref, wz_ref, wx_ref, wdt_ref, dtb_ref, z_ref, xbc_ref, dt_ref):
    xn = _rmsnorm(x_ref[...], g_ref[...]).astype(BF16)
    z_ref[...] = _dot(xn, wz_ref[...])
    xbc_ref[...] = _dot(xn, wx_ref[...])
    dt_ref[...] = _softplus(_dot(xn, wdt_ref[...]) + dtb_ref[...])


def _in_proj(x2, g, wz, wx, wdt, dtb, *, tm):
    T, D = x2.shape
    nz, nx, nd = wz.shape[1], wx.shape[1], wdt.shape[1]
    full = lambda i: (0, 0)
    return pl.pallas_call(
        _in_proj_kernel,
        grid=(T // tm,),
        in_specs=[
            pl.BlockSpec((tm, D), lambda i: (i, 0)),
            pl.BlockSpec((1, D), full),
            pl.BlockSpec((D, nz), full),
            pl.BlockSpec((D, nx), full),
            pl.BlockSpec((D, nd), full),
            pl.BlockSpec((1, nd), full),
        ],
        out_specs=[
            pl.BlockSpec((tm, nz), lambda i: (i, 0)),
            pl.BlockSpec((tm, nx), lambda i: (i, 0)),
            pl.BlockSpec((tm, nd), lambda i: (i, 0)),
        ],
        out_shape=[
            jax.ShapeDtypeStruct((T, nz), F32),
            jax.ShapeDtypeStruct((T, nx), F32),
            jax.ShapeDtypeStruct((T, nd), F32),
        ],
        compiler_params=_cparams(1),
        name="ssd_in_proj",
    )(x2, g, wz, wx, wdt, dtb)


def _ssd_kernel(xbc_ref, z_ref, dt_ref, cw_ref, cb_ref, a_ref, dsk_ref, gn_ref, yn_ref,
                state_ref, cbuf_ref, act_ref, yz_ref, *, Q, d_inner, n_groups, d_state, head_dim):
    c = pl.program_id(1)
    N = d_state
    gw = d_inner // n_groups
    pairs_per_group = gw // V7X_LANES
    halo = V7X_SUBLANES

    @pl.when(c == 0)
    def _init():
        state_ref[...] = jnp.zeros_like(state_ref)
        cbuf_ref[0:halo, :] = jnp.zeros((halo, cbuf_ref.shape[1]), F32)

    x_in = xbc_ref[...]
    cbuf_ref[halo:halo + Q, :] = x_in
    cw = cw_ref[...]
    conv = (cbuf_ref[halo - 3:halo - 3 + Q, :] * cw[0:1, :]
            + cbuf_ref[halo - 2:halo - 2 + Q, :] * cw[1:2, :]
            + cbuf_ref[halo - 1:halo - 1 + Q, :] * cw[2:3, :]
            + x_in * cw[3:4, :]) + cb_ref[...]
    cbuf_ref[0:halo, :] = x_in[Q - halo:Q, :]
    act_ref[...] = conv * _sigmoid(conv)

    dt = dt_ref[...]
    rows = lax.broadcasted_iota(I32, (Q, Q), 0)
    cols = lax.broadcasted_iota(I32, (Q, Q), 1)
    causal = rows >= cols
    tril = jnp.where(causal, 1.0, 0.0).astype(BF16)
    acum = _dot_exact_lhs(tril, dt * a_ref[...])
    acum_t = acum.T
    a_last = acum[Q - 1:Q, :]
    dte = jnp.exp(a_last - acum)
    din = jnp.exp(acum)
    cdec = jnp.exp(a_last)

    lane = lax.broadcasted_iota(I32, (Q, V7X_LANES), 1)
    lo_half = lane < head_dim
    lo_half_row = lo_half[0:1, :]

    def pair(v, h0, mask):
        return jnp.where(mask, v[:, h0:h0 + 1], v[:, h0 + 1:h0 + 2])

    for g in range(n_groups):
        b_f = act_ref[:, d_inner + g * N:d_inner + (g + 1) * N]
        b_g = b_f.astype(BF16)
        c_g = act_ref[:, d_inner + n_groups * N + g * N:d_inner + n_groups * N + (g + 1) * N].astype(BF16)
        cb = _dot_nt(c_g, b_g)
        s_old = state_ref[:, g * gw:(g + 1) * gw]
        y_off = _dot(c_g, s_old.astype(BF16))
        xw_parts = []
        cd_parts = []
        for jj in range(pairs_per_group):
            j = g * pairs_per_group + jj
            h0 = 2 * j
            sl = slice(j * V7X_LANES, (j + 1) * V7X_LANES)
            xs_p = act_ref[:, sl]
            xdt = xs_p * pair(dt, h0, lo_half)
            xdt_b = xdt.astype(BF16)
            ys = []
            for hh in (h0, h0 + 1):
                seg = acum[:, hh:hh + 1] - acum_t[hh:hh + 1, :]
                dec = jnp.exp(jnp.where(causal, seg, -jnp.inf))
                ys.append(_dot((cb * dec).astype(BF16), xdt_b))
            y = jnp.where(lo_half, ys[0], ys[1])
            y = y + y_off[:, jj * V7X_LANES:(jj + 1) * V7X_LANES] * pair(din, h0, lo_half)
            y = y + xs_p * dsk_ref[:, sl]
            zp = z_ref[:, sl]
            yz_ref[:, sl] = y * (zp * _sigmoid(zp))
            xw_parts.append((xdt * pair(dte, h0, lo_half)).astype(BF16))
            cd_parts.append(pair(cdec, h0, lo_half_row))
        xw_g = jnp.concatenate(xw_parts, axis=1)
        new_states = _dot(b_f.T.astype(BF16), xw_g)
        state_ref[:, g * gw:(g + 1) * gw] = s_old * jnp.concatenate(cd_parts, axis=1) + new_states
        yzg = yz_ref[:, g * gw:(g + 1) * gw]
        ms = jnp.mean(yzg * yzg, axis=-1, keepdims=True)
        yn_ref[:, g * gw:(g + 1) * gw] = (
            (yzg * lax.rsqrt(ms + EPS)) * gn_ref[:, g * gw:(g + 1) * gw]).astype(yn_ref.dtype)


def _ssd_scan(xbc, z, dt, cw, cb, a_row, dsk, gn, *, d_inner):
    B, L, cdim = xbc.shape
    Q = SSD_CHUNK
    nd = dt.shape[-1]
    full2 = lambda b, c: (0, 0)
    kern = functools.partial(_ssd_kernel, Q=Q, d_inner=d_inner, n_groups=SSD_N_GROUPS,
                             d_state=SSD_D_STATE, head_dim=SSD_HEAD_DIM)
    return pl.pallas_call(
        kern,
        grid=(B, L // Q),
        in_specs=[
            pl.BlockSpec((None, Q, cdim), lambda b, c: (b, c, 0)),
            pl.BlockSpec((None, Q, d_inner), lambda b, c: (b, c, 0)),
            pl.BlockSpec((None, Q, nd), lambda b, c: (b, c, 0)),
            pl.BlockSpec(cw.shape, full2),
            pl.BlockSpec(cb.shape, full2),
            pl.BlockSpec(a_row.shape, full2),
            pl.BlockSpec(dsk.shape, full2),
            pl.BlockSpec(gn.shape, full2),
        ],
        out_specs=pl.BlockSpec((None, Q, d_inner), lambda b, c: (b, c, 0)),
        out_shape=jax.ShapeDtypeStruct((B, L, d_inner), BF16),
        scratch_shapes=[
            pltpu.VMEM((SSD_D_STATE, d_inner), F32),
            pltpu.VMEM((Q + V7X_SUBLANES, cdim), F32),
            pltpu.VMEM((Q, cdim), F32),
            pltpu.VMEM((Q, d_inner), F32),
        ],
        compiler_params=_cparams(2),
        name="ssd_scan",
    )(xbc, z, dt, cw, cb, a_row, dsk, gn)


def _proj_res_kernel(a_ref, w_ref, r_ref, o_ref):
    o_ref[...] = r_ref[...] + _dot(a_ref[...], w_ref[...])


def _proj_res(a, w, res, *, tm):
    T, K = a.shape
    N = w.shape[1]
    return pl.pallas_call(
        _proj_res_kernel,
        grid=(T // tm,),
        in_specs=[
            pl.BlockSpec((tm, K), lambda i: (i, 0)),
            pl.BlockSpec((K, N), lambda i: (0, 0)),
            pl.BlockSpec((tm, N), lambda i: (i, 0)),
        ],
        out_specs=pl.BlockSpec((tm, N), lambda i: (i, 0)),
        out_shape=jax.ShapeDtypeStruct((T, N), F32),
        compiler_params=_cparams(1),
        name="proj_residual",
    )(a, w, res)


def _qkv_kernel(x_ref, g_ref, w_ref, q_ref, k_ref, v_ref, *, n_heads, hd, scale):
    xn = _rmsnorm(x_ref[...], g_ref[...]).astype(BF16)
    qkv = _dot(xn, w_ref[...])
    d = n_heads * hd
    for h in range(n_heads):
        q_ref[h] = (qkv[:, h * hd:(h + 1) * hd] * scale).astype(BF16)
        k_ref[h] = qkv[:, d + h * hd:d + (h + 1) * hd].astype(BF16)
        v_ref[h] = qkv[:, 2 * d + h * hd:2 * d + (h + 1) * hd].astype(BF16)


def _qkv_proj(x, g, w, *, n_heads, hd, tm):
    B, L, D = x.shape
    kern = functools.partial(_qkv_kernel, n_heads=n_heads, hd=hd, scale=1.0 / math.sqrt(hd))
    hspec = pl.BlockSpec((None, n_heads, tm, hd), lambda b, i: (b, 0, i, 0))
    hshape = jax.ShapeDtypeStruct((B, n_heads, L, hd), BF16)
    return pl.pallas_call(
        kern,
        grid=(B, L // tm),
        in_specs=[
            pl.BlockSpec((None, tm, D), lambda b, i: (b, i, 0)),
            pl.BlockSpec((1, D), lambda b, i: (0, 0)),
            pl.BlockSpec(w.shape, lambda b, i: (0, 0)),
        ],
        out_specs=[hspec, hspec, hspec],
        out_shape=[hshape, hshape, hshape],
        compiler_params=_cparams(2),
        name="sb_qkv_proj",
    )(x, g, w)


def _attn_kernel(q_ref, k_ref, v_ref, o_ref, acc_ref, carry_ref, *, tq):
    qi = pl.program_id(1)
    tk = tq
    q = q_ref[...]
    rows = lax.broadcasted_iota(I32, (tq, tk), 0)
    cols = lax.broadcasted_iota(I32, (tq, tk), 1)
    before = cols < rows
    tri = jnp.where(rows >= cols, 1.0, 0.0).astype(BF16)
    tri2 = jnp.concatenate([tri, tri], axis=0)

    def block(kb, diag):
        ks = pl.multiple_of(kb * tk, tk)
        k = k_ref[pl.ds(ks, tk), :]
        v = v_ref[pl.ds(ks, tk), :]
        z = _dot_nt(q, k)
        sp = _softplus(z)
        if diag:
            sp = jnp.where(before, sp, 0.0)
        hi, lo = _split2(sp)
        cs = _dot(jnp.concatenate([hi, lo], axis=1), tri2)
        carry = carry_ref[:, 0:1]
        log_a = z - cs + carry
        if diag:
            log_a = jnp.where(before, log_a, -jnp.inf)
        a = jnp.exp(log_a)
        acc_ref[...] += _dot(a.astype(BF16), v)
        new_carry = carry - cs[:, 0:1]
        carry_ref[...] = jnp.broadcast_to(new_carry, carry_ref.shape)
        return jnp.max(new_carry)

    acc_ref[...] = jnp.zeros_like(acc_ref)
    carry_ref[...] = jnp.zeros_like(carry_ref)
    live0 = block(qi, True)

    def cond(st):
        kb, live = st
        return jnp.logical_and(kb >= 0, live > ATTN_DEAD_LOG)

    def body(st):
        kb, _ = st
        return kb - 1, block(kb, False)

    lax.while_loop(cond, body, (qi - 1, live0))
    o_ref[...] = acc_ref[...].astype(o_ref.dtype)


def _attention(q, k, v, *, B, n_heads, tq):
    BH, L, hd = q.shape
    kern = functools.partial(_attn_kernel, tq=tq)
    return pl.pallas_call(
        kern,
        grid=(BH, L // tq),
        in_specs=[
            pl.BlockSpec((None, tq, hd), lambda bh, i: (bh, i, 0)),
            pl.BlockSpec((None, L, hd), lambda bh, i: (bh, 0, 0)),
            pl.BlockSpec((None, L, hd), lambda bh, i: (bh, 0, 0)),
        ],
        out_specs=pl.BlockSpec((None, tq, hd), lambda bh, i: (bh // n_heads, i, bh % n_heads)),
        out_shape=jax.ShapeDtypeStruct((B, L, n_heads * hd), BF16),
        scratch_shapes=[pltpu.VMEM((tq, hd), F32), pltpu.VMEM((tq, V7X_LANES), F32)],
        compiler_params=_cparams(2),
        name="sb_attention",
    )(q, k, v)


RI_E0, RI_E1, RI_G0, RI_G1, RI_R0, RI_R1 = 0, 1, 2, 3, 4, 5


def _router_kernel(h_ref, g_ref, w_ref, b_ref, ri_ref, cnt_ref, carry_ref, *, n_groups, n_experts):
    i = pl.program_id(0)
    tm = h_ref.shape[0]
    epg = n_experts // n_groups

    @pl.when(i == 0)
    def _init():
        carry_ref[...] = jnp.zeros_like(carry_ref)

    xn = _rmsnorm(h_ref[...], g_ref[...])
    xh, xm, xl = _split3(xn)
    w = w_ref[...]
    wh, wm, wl = _split3(w)
    logits = (_dot(xh, wh) + (_dot(xh, wm) + _dot(xm, wh))
              + (_dot(xm, wm) + _dot(xh, wl) + _dot(xl, wh))) + b_ref[...]

    lane = lax.broadcasted_iota(I32, (tm, V7X_LANES), 1).astype(F32)
    neg = -jnp.inf
    no_lane = float(V7X_LANES)
    gl = jnp.where(lane < n_groups, logits, neg)
    gmax = jnp.max(gl, axis=-1, keepdims=True)
    gidx = jnp.min(jnp.where(gl == gmax, lane, no_lane), axis=-1, keepdims=True)
    g_w = 1.0 / jnp.sum(jnp.exp(gl - gmax), axis=-1, keepdims=True)

    first = n_groups + gidx * epg
    in_group = jnp.logical_and(lane >= first, lane < first + epg)
    el = jnp.where(in_group, logits, neg)
    m0 = jnp.max(el, axis=-1, keepdims=True)
    i0 = jnp.min(jnp.where(el == m0, lane, no_lane), axis=-1, keepdims=True)
    el1 = jnp.where(lane == i0, neg, el)
    m1 = jnp.max(el1, axis=-1, keepdims=True)
    i1 = jnp.min(jnp.where(el1 == m1, lane, no_lane), axis=-1, keepdims=True)
    d = jnp.exp(m1 - m0)
    p0 = 1.0 / (1.0 + d)
    gate0 = g_w * p0
    gate1 = g_w * (d * p0)

    oh0 = jnp.where(lane == i0, 1.0, 0.0)
    oh1 = jnp.where(lane == i1, 1.0, 0.0)
    r = lax.broadcasted_iota(I32, (tm, tm), 0)
    cc = lax.broadcasted_iota(I32, (tm, tm), 1)
    strict = jnp.where(cc < r, 1.0, 0.0).astype(BF16)
    pre0 = _dot(strict, oh0.astype(BF16))
    pre1 = _dot(strict, oh1.astype(BF16))
    cnt0 = jnp.sum(oh0, axis=0, keepdims=True)
    cnt1 = jnp.sum(oh1, axis=0, keepdims=True)
    base = carry_ref[0:1, :]
    rank0 = jnp.sum(oh0 * (pre0 + base), axis=-1, keepdims=True)
    rank1 = jnp.sum(oh1 * (pre1 + (base + cnt0)), axis=-1, keepdims=True)
    total = base + cnt0 + cnt1
    carry_ref[...] = jnp.broadcast_to(total, carry_ref.shape)
    cnt_ref[...] = jnp.broadcast_to(total, cnt_ref.shape)

    e0 = i0 - n_groups
    e1 = i1 - n_groups
    ri = jnp.where(lane == RI_E0, e0, 0.0)
    ri = jnp.where(lane == RI_E1, e1, ri)
    ri = jnp.where(lane == RI_G0, gate0, ri)
    ri = jnp.where(lane == RI_G1, gate1, ri)
    ri = jnp.where(lane == RI_R0, rank0, ri)
    ri = jnp.where(lane == RI_R1, rank1, ri)
    ri_ref[...] = ri


def _router(h2, g, w_r, b_r, *, n_experts, tm):
    T, D = h2.shape
    kern = functools.partial(_router_kernel, n_groups=MOE_GROUPS, n_experts=n_experts)
    return pl.pallas_call(
        kern,
        grid=(T // tm,),
        in_specs=[
            pl.BlockSpec((tm, D), lambda i: (i, 0)),
            pl.BlockSpec((1, D), lambda i: (0, 0)),
            pl.BlockSpec((D, V7X_LANES), lambda i: (0, 0)),
            pl.BlockSpec((1, V7X_LANES), lambda i: (0, 0)),
        ],
        out_specs=[
            pl.BlockSpec((tm, V7X_LANES), lambda i: (i, 0)),
            pl.BlockSpec((V7X_SUBLANES, V7X_LANES), lambda i: (0, 0)),
        ],
        out_shape=[
            jax.ShapeDtypeStruct((T, V7X_LANES), F32),
            jax.ShapeDtypeStruct((V7X_SUBLANES, V7X_LANES), F32),
        ],
        scratch_shapes=[pltpu.VMEM((V7X_SUBLANES, V7X_LANES), F32)],
        compiler_params=_cparams(1),
        name="moe_router",
    )(h2, g, w_r, b_r)


def _row_copy(src_ref, src_row, dst_ref, dst_row, sem):
    return pltpu.make_async_copy(src_ref.at[pl.ds(src_row, 1)], dst_ref.at[pl.ds(dst_row, 1)], sem)


def _dispatch_kernel(dest_ref, h_ref, g_ref, init_ref, rows_ref, xn_ref, sem):
    del init_ref
    tm = h_ref.shape[0]
    xn_ref[...] = _rmsnorm(h_ref[...], g_ref[...])

    def issue(t, c):
        for k in range(MOE_TOP_K):
            _row_copy(xn_ref, t, rows_ref, dest_ref[0, 0, MOE_TOP_K * t + k], sem).start()
        return c

    lax.fori_loop(0, tm, issue, 0, unroll=DMA_UNROLL)

    def drain(t, c):
        for k in range(MOE_TOP_K):
            _row_copy(xn_ref, t, rows_ref, dest_ref[0, 0, MOE_TOP_K * t + k], sem).wait()
        return c

    lax.fori_loop(0, tm, drain, 0, unroll=DMA_UNROLL)


def _dispatch(dest3, h2, g, n_rows, *, tm):
    T, D = h2.shape
    init = jnp.zeros((n_rows, D), F32)
    return pl.pallas_call(
        _dispatch_kernel,
        grid=(T // tm,),
        in_specs=[
            pl.BlockSpec((1, 1, MOE_TOP_K * tm), lambda i: (i, 0, 0), memory_space=pltpu.SMEM),
            pl.BlockSpec((tm, D), lambda i: (i, 0)),
            pl.BlockSpec((1, D), lambda i: (0, 0)),
            pl.BlockSpec(memory_space=pl.ANY),
        ],
        out_specs=pl.BlockSpec(memory_space=pl.ANY),
        out_shape=jax.ShapeDtypeStruct((n_rows, D), F32),
        scratch_shapes=[pltpu.VMEM((tm, D), F32), pltpu.SemaphoreType.DMA(())],
        input_output_aliases={3: 0},
        compiler_params=_cparams(1),
        name="moe_dispatch",
    )(dest3, h2, g, init)


def _expert_kernel(be_ref, nu_ref, x_ref, wg_ref, wu_ref, wd_ref, y_ref):
    i = pl.program_id(0)

    @pl.when(i < nu_ref[0])
    def _live():
        x = x_ref[...].astype(BF16)
        gt = _dot(x, wg_ref[0].astype(BF16))
        up = _dot(x, wu_ref[0].astype(BF16))
        act = (gt * _sigmoid(gt)) * up
        y_ref[...] = _dot(act.astype(BF16), wd_ref[0].astype(BF16))

    @pl.when(i >= nu_ref[0])
    def _dead():
        y_ref[...] = jnp.zeros_like(y_ref)


def _experts(blk_expert, n_used, x_rows, w_gate, w_up, w_down, *, layer, rb):
    n_rows, D = x_rows.shape
    _, E, _, F = w_gate.shape
    n_blk = n_rows // rb

    def row_map(i, be, nu):
        return (jnp.minimum(i, nu[0] - 1), 0)

    def w_map(i, be, nu):
        return (layer, be[i], 0, 0)

    grid_spec = pltpu.PrefetchScalarGridSpec(
        num_scalar_prefetch=2,
        grid=(n_blk,),
        in_specs=[
            pl.BlockSpec((rb, D), row_map),
            pl.BlockSpec((None, 1, D, F), w_map),
            pl.BlockSpec((None, 1, D, F), w_map),
            pl.BlockSpec((None, 1, F, D), w_map),
        ],
        out_specs=pl.BlockSpec((rb, D), lambda i, be, nu: (i, 0)),
    )
    return pl.pallas_call(
        _expert_kernel,
        grid_spec=grid_spec,
        out_shape=jax.ShapeDtypeStruct((n_rows, D), F32),
        compiler_params=_cparams(1),
        name="moe_experts",
    )(blk_expert, n_used, x_rows, w_gate, w_up, w_down)


def _combine_kernel(dest_ref, h_ref, ri_ref, p_ref, yrows_ref, pn_ref, wg_ref, bg_ref, wp_ref, fn_ref,
                    o_ref, ybuf_ref, sem, *, final_norm):
    tm = h_ref.shape[0]

    def issue(t, c):
        for k in range(MOE_TOP_K):
            _row_copy(yrows_ref, dest_ref[0, 0, MOE_TOP_K * t + k], ybuf_ref.at[k], t, sem).start()
        return c

    lax.fori_loop(0, tm, issue, 0, unroll=DMA_UNROLL)

    def drain(t, c):
        for k in range(MOE_TOP_K):
            _row_copy(yrows_ref, dest_ref[0, 0, MOE_TOP_K * t + k], ybuf_ref.at[k], t, sem).wait()
        return c

    lax.fori_loop(0, tm, drain, 0, unroll=DMA_UNROLL)

    ri = ri_ref[...]
    h = h_ref[...] + (ybuf_ref[0] * ri[:, RI_G0:RI_G0 + 1] + ybuf_ref[1] * ri[:, RI_G1:RI_G1 + 1])
    xn = _rmsnorm(h, pn_ref[...]).astype(BF16)
    gate = _sigmoid(_dot(xn, wg_ref[...]) + bg_ref[...])
    h = h + gate * _dot(p_ref[...].astype(BF16), wp_ref[...])
    if final_norm:
        h = _rmsnorm(h, fn_ref[...])
    o_ref[...] = h


def _combine(dest3, h2, ri, p3, y_rows, pn, wg, bg, wp, fnorm, *, layer, tm, final_norm):
    T, D = h2.shape
    P = p3.shape[2]
    kern = functools.partial(_combine_kernel, final_norm=final_norm)
    row = lambda i: (0, 0)
    return pl.pallas_call(
        kern,
        grid=(T // tm,),
        in_specs=[
            pl.BlockSpec((1, 1, MOE_TOP_K * tm), lambda i: (i, 0, 0), memory_space=pltpu.SMEM),
            pl.BlockSpec((tm, D), lambda i: (i, 0)),
            pl.BlockSpec((tm, V7X_LANES), lambda i: (i, 0)),
            pl.BlockSpec((None, tm, P), lambda i: (layer, i, 0)),
            pl.BlockSpec(memory_space=pl.ANY),
            pl.BlockSpec((1, D), row),
            pl.BlockSpec((D, D), row),
            pl.BlockSpec((1, D), row),
            pl.BlockSpec((P, D), row),
            pl.BlockSpec((1, D), row),
        ],
        out_specs=pl.BlockSpec((tm, D), lambda i: (i, 0)),
        out_shape=jax.ShapeDtypeStruct((T, D), F32),
        scratch_shapes=[pltpu.VMEM((MOE_TOP_K, tm, D), F32), pltpu.SemaphoreType.DMA(())],
        compiler_params=_cparams(1),
        name="moe_combine_ple",
    )(dest3, h2, ri, p3, y_rows, pn, wg, bg, wp, fnorm)


TM_PROJ = 256
TM_RES = 512
TM_TOKEN = 256
EXPERT_ROWS = 256
ATTN_BLOCK = 256


def _row(v):
    return v.reshape(1, -1).astype(F32)


def _moe_ple(h2, p3, moe_norm, w_rg, b_rg, w_re, b_re, w_gate, w_up, w_down,
             ple_norm, ple_w_gate, ple_b_gate, ple_w_proj, final_norm, *, layer, last):
    T, D = h2.shape
    G = w_rg.shape[1]
    E = w_re.shape[1]
    pad = V7X_LANES - G - E
    w_r = jnp.concatenate([w_rg, w_re, jnp.zeros((D, pad), F32)], axis=1)
    b_r = jnp.concatenate([b_rg, b_re, jnp.zeros((pad,), F32)]).reshape(1, V7X_LANES)
    ri, cnt = _router(h2, _row(moe_norm), w_r, b_r, n_experts=E, tm=TM_TOKEN)

    rb = EXPERT_ROWS
    counts = cnt[0, G:G + E].astype(I32)
    padded = ((counts + rb - 1) // rb) * rb
    pad_end = jnp.cumsum(padded)
    pad_start = pad_end - padded
    n_blk = (T * MOE_TOP_K) // rb + E
    n_rows = n_blk * rb
    expert = ri[:, RI_E0:RI_E1 + 1].astype(I32)
    rank = ri[:, RI_R0:RI_R1 + 1].astype(I32)
    eids = jnp.arange(E, dtype=I32)
    dest = jnp.sum(jnp.where(expert[:, :, None] == eids, pad_start, 0), axis=-1) + rank
    dest3 = dest.reshape(T // TM_TOKEN, 1, MOE_TOP_K * TM_TOKEN)
    blk_row0 = jnp.arange(n_blk, dtype=I32) * rb
    blk_expert = jnp.minimum(jnp.sum((pad_end[None, :] <= blk_row0[:, None]).astype(I32), axis=1), E - 1)
    n_used = (pad_end[-1:] // rb).astype(I32)

    x_rows = _dispatch(dest3, h2, _row(moe_norm), n_rows, tm=TM_TOKEN)
    y_rows = _experts(blk_expert, n_used, x_rows, w_gate, w_up, w_down, layer=layer, rb=rb)
    return _combine(dest3, h2, ri, p3, y_rows, _row(ple_norm), ple_w_gate.astype(BF16),
                    _row(ple_b_gate), ple_w_proj.astype(BF16), _row(final_norm),
                    layer=layer, tm=TM_TOKEN, final_norm=last)


def _ssd_layer(h2, B, L, norm, w_in, conv_w, conv_b, dt_bias, a_log, d_skip, gnorm, w_out):
    T, D = h2.shape
    H = a_log.shape[0]
    d_inner = H * SSD_HEAD_DIM
    cdim = conv_w.shape[1]
    wz = w_in[:, :d_inner].astype(BF16)
    wx = w_in[:, d_inner:d_inner + cdim].astype(BF16)
    padh = V7X_LANES - H
    wdt = jnp.concatenate([w_in[:, d_inner + cdim:], jnp.zeros((D, padh), F32)], axis=1).astype(BF16)
    dtb = jnp.concatenate([dt_bias, jnp.zeros((padh,), F32)]).reshape(1, V7X_LANES)
    a_row = jnp.concatenate([-jnp.exp(a_log.astype(F32)), jnp.zeros((padh,), F32)]).reshape(1, V7X_LANES)
    dsk = jnp.repeat(d_skip.astype(F32), SSD_HEAD_DIM).reshape(1, d_inner)

    z, xbc, dt = _in_proj(h2, _row(norm), wz, wx, wdt, dtb, tm=TM_PROJ)
    yn = _ssd_scan(xbc.reshape(B, L, cdim), z.reshape(B, L, d_inner), dt.reshape(B, L, V7X_LANES),
                   conv_w.astype(F32), _row(conv_b), a_row, dsk, _row(gnorm), d_inner=d_inner)
    return _proj_res(yn.reshape(T, d_inner), w_out.astype(BF16), h2, tm=TM_RES)


def _sb_layer(h2, B, L, norm, w_qkv, w_o):
    T, D = h2.shape
    hd = SB_HEAD_DIM
    n_heads = w_o.shape[0] // hd
    q, k, v = _qkv_proj(h2.reshape(B, L, D), _row(norm), w_qkv.astype(BF16),
                        n_heads=n_heads, hd=hd, tm=TM_PROJ)
    bh = B * n_heads
    o = _attention(q.reshape(bh, L, hd), k.reshape(bh, L, hd), v.reshape(bh, L, hd),
                   B=B, n_heads=n_heads, tq=ATTN_BLOCK)
    return _proj_res(o.reshape(T, n_heads * hd), w_o.astype(BF16), h2, tm=TM_RES)


def kernel(x, p, ssd_norm, ssd_w_in, ssd_conv_w, ssd_conv_b, ssd_dt_bias, ssd_a_log, ssd_d, ssd_gnorm, ssd_w_out, sb_norm, sb_w_qkv, sb_w_o, moe_norm, moe_w_rg, moe_b_rg, moe_w_re, moe_b_re, moe_w_gate, moe_w_up, moe_w_down, ple_norm, ple_w_gate, ple_b_gate, ple_w_proj, final_norm):
    B, L, D = x.shape
    depth = p.shape[0]
    T = B * L
    n_mixers = 2
    h = x.reshape(T, D)
    p3 = p.reshape(depth, T, p.shape[-1])
    for i in range(depth):
        j = i // n_mixers
        if i % n_mixers == 0:
            h = _ssd_layer(h, B, L, ssd_norm[j], ssd_w_in[j], ssd_conv_w[j], ssd_conv_b[j], ssd_dt_bias[j],
                           ssd_a_log[j], ssd_d[j], ssd_gnorm[j], ssd_w_out[j])
        else:
            h = _sb_layer(h, B, L, sb_norm[j], sb_w_qkv[j], sb_w_o[j])
        h = _moe_ple(h, p3, moe_norm[i], moe_w_rg[i], moe_b_rg[i], moe_w_re[i], moe_b_re[i],
                     moe_w_gate, moe_w_up, moe_w_down, ple_norm[i], ple_w_gate[i], ple_b_gate[i],
                     ple_w_proj[i], final_norm, layer=i, last=(i == depth - 1))
    return h.reshape(B, L, D)
```

```python
import functools
import math

import jax
import jax.numpy as jnp
from jax import lax
from jax.experimental import pallas as pl
from jax.experimental.pallas import tpu as pltpu

F32 = jnp.float32
BF16 = jnp.bfloat16
I32 = jnp.int32

EPS = 1e-6
V7X_LANES = 128
V7X_SUBLANES = 8
V7X_VMEM_LIMIT_BYTES = 56 * 1024 * 1024

SSD_HEAD_DIM = 64
SSD_N_GROUPS = 4
SSD_D_STATE = 128
SSD_CHUNK = 128
SSD_CONV = 4
SB_HEAD_DIM = 128
MOE_GROUPS = 8
MOE_TOP_K = 2

SCALAR_UNROLL = 8

ATTN_DEAD_LOG = -110.0


def _cparams(n_axes):
    return pltpu.CompilerParams(
        dimension_semantics=("arbitrary",) * n_axes,
        vmem_limit_bytes=V7X_VMEM_LIMIT_BYTES,
    )


def _dot(a, b):
    return jnp.dot(a, b, preferred_element_type=F32)


def _dot_nt(a, b):
    return lax.dot_general(a, b, (((1,), (1,)), ((), ())), preferred_element_type=F32)


def _split3(x):
    hi = x.astype(BF16)
    r1 = x - hi.astype(F32)
    mid = r1.astype(BF16)
    lo = (r1 - mid.astype(F32)).astype(BF16)
    return hi, mid, lo


def _split2(x):
    hi = x.astype(BF16)
    lo = (x - hi.astype(F32)).astype(BF16)
    return hi, lo


def _dot_exact_lhs(a_bf16, x):
    hi, mid, lo = _split3(x)
    return _dot(a_bf16, hi) + _dot(a_bf16, mid) + _dot(a_bf16, lo)


def _sigmoid(x):
    return 1.0 / (1.0 + jnp.exp(-x))


def _softplus(x):
    return jnp.maximum(x, 0.0) + jnp.log(1.0 + jnp.exp(-jnp.abs(x)))


def _rmsnorm(x, g):
    ms = jnp.mean(x * x, axis=-1, keepdims=True)
    return (x * lax.rsqrt(ms + EPS)) * g


def _tile_rows_load(ref, n):
    return jnp.concatenate(
        [ref[pl.ds(j, n, stride=V7X_SUBLANES), :] for j in range(V7X_SUBLANES)], axis=1)


def _tile_rows_store(ref, val, n):
    for j in range(V7X_SUBLANES):
        ref[pl.ds(j, n, stride=V7X_SUBLANES), :] = val[:, j * V7X_LANES:(j + 1) * V7X_LANES]


def _in_proj_kernel(x_ref, g_ref, wz_ref, wx_ref, wdt_ref, dtb_ref, z_ref, xbc_ref, dt_ref):
    xn = _rmsnorm(x_ref[...], g_ref[...]).astype(BF16)
    z_ref[...] = _dot(xn, wz_ref[...])
    xbc_ref[...] = _dot(xn, wx_ref[...])
    dt_ref[...] = _softplus(_dot(xn, wdt_ref[...]) + dtb_ref[...])


def _in_proj(x2, g, wz, wx, wdt, dtb, *, tm):
    T, D = x2.shape
    nz, nx, nd = wz.shape[1], wx.shape[1], wdt.shape[1]
    full = lambda i: (0, 0)
    return pl.pallas_call(
        _in_proj_kernel,
        grid=(T // tm,),
        in_specs=[
            pl.BlockSpec((tm, D), lambda i: (i, 0)),
            pl.BlockSpec((1, D), full),
            pl.BlockSpec((D, nz), full),
            pl.BlockSpec((D, nx), full),
            pl.BlockSpec((D, nd), full),
            pl.BlockSpec((1, nd), full),
        ],
        out_specs=[
            pl.BlockSpec((tm, nz), lambda i: (i, 0)),
            pl.BlockSpec((tm, nx), lambda i: (i, 0)),
            pl.BlockSpec((tm, nd), lambda i: (i, 0)),
        ],
        out_shape=[
            jax.ShapeDtypeStruct((T, nz), F32),
            jax.ShapeDtypeStruct((T, nx), F32),
            jax.ShapeDtypeStruct((T, nd), F32),
        ],
        compiler_params=_cparams(1),
        name="ssd_in_proj",
    )(x2, g, wz, wx, wdt, dtb)


def _ssd_kernel(xbc_ref, z_ref, dt_ref, cw_ref, cb_ref, a_ref, dsk_ref, gn_ref, yn_ref,
                state_ref, cbuf_ref, act_ref, yz_ref, *, Q, d_inner, n_groups, d_state, head_dim):
    c = pl.program_id(1)
    N = d_state
    gw = d_inner // n_groups
    pairs_per_group = gw // V7X_LANES
    halo = V7X_SUBLANES

    @pl.when(c == 0)
    def _init():
        state_ref[...] = jnp.zeros_like(state_ref)
        cbuf_ref[0:halo, :] = jnp.zeros((halo, cbuf_ref.shape[1]), F32)

    x_in = xbc_ref[...]
    cbuf_ref[halo:halo + Q, :] = x_in
    cw = cw_ref[...]
    conv = x_in * cw[SSD_CONV - 1:SSD_CONV, :] + cb_ref[...]
    for k in range(SSD_CONV - 1):
        off = halo - (SSD_CONV - 1) + k
        conv = conv + cbuf_ref[off:off + Q, :] * cw[k:k + 1, :]
    cbuf_ref[0:halo, :] = x_in[Q - halo:Q, :]
    act_ref[...] = conv * _sigmoid(conv)

    dt = dt_ref[...]
    rows = lax.broadcasted_iota(I32, (Q, Q), 0)
    cols = lax.broadcasted_iota(I32, (Q, Q), 1)
    causal = rows >= cols
    tril = jnp.where(causal, 1.0, 0.0).astype(BF16)
    acum = _dot_exact_lhs(tril, dt * a_ref[...])
    acum_t = acum.T
    a_last = acum[Q - 1:Q, :]
    dte = jnp.exp(a_last - acum)
    din = jnp.exp(acum)
    cdec = jnp.exp(a_last)

    lane = lax.broadcasted_iota(I32, (Q, V7X_LANES), 1)
    lo_half = lane < head_dim
    lo_half_row = lo_half[0:1, :]

    def pair(v, h0, mask):
        return jnp.where(mask, v[:, h0:h0 + 1], v[:, h0 + 1:h0 + 2])

    for g in range(n_groups):
        b_f = act_ref[:, d_inner + g * N:d_inner + (g + 1) * N]
        b_g = b_f.astype(BF16)
        c_g = act_ref[:, d_inner + n_groups * N + g * N:d_inner + n_groups * N + (g + 1) * N].astype(BF16)
        cb = _dot_nt(c_g, b_g)
        s_old = state_ref[:, g * gw:(g + 1) * gw]
        y_off = _dot(c_g, s_old.astype(BF16))
        xw_parts = []
        cd_parts = []
        for jj in range(pairs_per_group):
            j = g * pairs_per_group + jj
            h0 = 2 * j
            sl = slice(j * V7X_LANES, (j + 1) * V7X_LANES)
            xs_p = act_ref[:, sl]
            xdt = xs_p * pair(dt, h0, lo_half)
            xdt_b = xdt.astype(BF16)
            ys = []
            for hh in (h0, h0 + 1):
                seg = acum[:, hh:hh + 1] - acum_t[hh:hh + 1, :]
                dec = jnp.exp(jnp.where(causal, seg, -jnp.inf))
                ys.append(_dot((cb * dec).astype(BF16), xdt_b))
            y = jnp.where(lo_half, ys[0], ys[1])
            y = y + y_off[:, jj * V7X_LANES:(jj + 1) * V7X_LANES] * pair(din, h0, lo_half)
            y = y + xs_p * dsk_ref[:, sl]
            zp = z_ref[:, sl]
            yz_ref[:, sl] = y * (zp * _sigmoid(zp))
            xw_parts.append((xdt * pair(dte, h0, lo_half)).astype(BF16))
            cd_parts.append(pair(cdec, h0, lo_half_row))
        xw_g = jnp.concatenate(xw_parts, axis=1)
        new_states = _dot(b_f.T.astype(BF16), xw_g)
        state_ref[:, g * gw:(g + 1) * gw] = s_old * jnp.concatenate(cd_parts, axis=1) + new_states
        yzg = yz_ref[:, g * gw:(g + 1) * gw]
        ms = jnp.mean(yzg * yzg, axis=-1, keepdims=True)
        yn_ref[:, g * gw:(g + 1) * gw] = (
            (yzg * lax.rsqrt(ms + EPS)) * gn_ref[:, g * gw:(g + 1) * gw]).astype(yn_ref.dtype)


def _ssd_scan(xbc, z, dt, cw, cb, a_row, dsk, gn, *, d_inner):
    B, L, cdim = xbc.shape
    Q = SSD_CHUNK
    nd = dt.shape[-1]
    full2 = lambda b, c: (0, 0)
    kern = functools.partial(_ssd_kernel, Q=Q, d_inner=d_inner, n_groups=SSD_N_GROUPS,
                             d_state=SSD_D_STATE, head_dim=SSD_HEAD_DIM)
    return pl.pallas_call(
        kern,
        grid=(B, L // Q),
        in_specs=[
            pl.BlockSpec((None, Q, cdim), lambda b, c: (b, c, 0)),
            pl.BlockSpec((None, Q, d_inner), lambda b, c: (b, c, 0)),
            pl.BlockSpec((None, Q, nd), lambda b, c: (b, c, 0)),
            pl.BlockSpec(cw.shape, full2),
            pl.BlockSpec(cb.shape, full2),
            pl.BlockSpec(a_row.shape, full2),
            pl.BlockSpec(dsk.shape, full2),
            pl.BlockSpec(gn.shape, full2),
        ],
        out_specs=pl.BlockSpec((None, Q, d_inner), lambda b, c: (b, c, 0)),
        out_shape=jax.ShapeDtypeStruct((B, L, d_inner), BF16),
        scratch_shapes=[
            pltpu.VMEM((SSD_D_STATE, d_inner), F32),
            pltpu.VMEM((Q + V7X_SUBLANES, cdim), F32),
            pltpu.VMEM((Q, cdim), F32),
            pltpu.VMEM((Q, d_inner), F32),
        ],
        compiler_params=_cparams(2),
        name="ssd_scan",
    )(xbc, z, dt, cw, cb, a_row, dsk, gn)


def _proj_res_kernel(a_ref, w_ref, r_ref, o_ref):
    o_ref[...] = r_ref[...] + _dot(a_ref[...], w_ref[...])


def _proj_res(a, w, res, *, tm):
    T, K = a.shape
    N = w.shape[1]
    return pl.pallas_call(
        _proj_res_kernel,
        grid=(T // tm,),
        in_specs=[
            pl.BlockSpec((tm, K), lambda i: (i, 0)),
            pl.BlockSpec((K, N), lambda i: (0, 0)),
            pl.BlockSpec((tm, N), lambda i: (i, 0)),
        ],
        out_specs=pl.BlockSpec((tm, N), lambda i: (i, 0)),
        out_shape=jax.ShapeDtypeStruct((T, N), F32),
        compiler_params=_cparams(1),
        name="proj_residual",
    )(a, w, res)


def _qkv_kernel(x_ref, g_ref, w_ref, q_ref, k_ref, v_ref, *, n_heads, hd, scale):
    xn = _rmsnorm(x_ref[...], g_ref[...]).astype(BF16)
    qkv = _dot(xn, w_ref[...])
    d = n_heads * hd
    for h in range(n_heads):
        q_ref[h] = (qkv[:, h * hd:(h + 1) * hd] * scale).astype(BF16)
        k_ref[h] = qkv[:, d + h * hd:d + (h + 1) * hd].astype(BF16)
        v_ref[h] = qkv[:, 2 * d + h * hd:2 * d + (h + 1) * hd].astype(BF16)


def _qkv_proj(x, g, w, *, n_heads, hd, tm):
    B, L, D = x.shape
    kern = functools.partial(_qkv_kernel, n_heads=n_heads, hd=hd, scale=1.0 / math.sqrt(hd))
    hspec = pl.BlockSpec((None, n_heads, tm, hd), lambda b, i: (b, 0, i, 0))
    hshape = jax.ShapeDtypeStruct((B, n_heads, L, hd), BF16)
    return pl.pallas_call(
        kern,
        grid=(B, L // tm),
        in_specs=[
            pl.BlockSpec((None, tm, D), lambda b, i: (b, i, 0)),
            pl.BlockSpec((1, D), lambda b, i: (0, 0)),
            pl.BlockSpec(w.shape, lambda b, i: (0, 0)),
        ],
        out_specs=[hspec, hspec, hspec],
        out_shape=[hshape, hshape, hshape],
        compiler_params=_cparams(2),
        name="sb_qkv_proj",
    )(x, g, w)


def _attn_kernel(q_ref, k_ref, v_ref, o_ref, acc_ref, carry_ref, *, tq, tk):
    qi = pl.program_id(1)
    n_diag = tq // tk
    rows_t = lax.broadcasted_iota(I32, (tk, tk), 0)
    cols_t = lax.broadcasted_iota(I32, (tk, tk), 1)
    tri = jnp.where(rows_t >= cols_t, 1.0, 0.0).astype(BF16)
    tri2 = jnp.concatenate([tri, tri], axis=0)

    def block(k0, r0, diag):
        nr = tq - r0
        q = q_ref[r0:tq, :]
        k = k_ref[pl.ds(k0, tk), :]
        v = v_ref[pl.ds(k0, tk), :]
        z = _dot_nt(q, k)
        sp = _softplus(z)
        if diag:
            before = (lax.broadcasted_iota(I32, (nr, tk), 1) < lax.broadcasted_iota(I32, (nr, tk), 0))
            sp = jnp.where(before, sp, 0.0)
        hi, lo = _split2(sp)
        cs = _dot(jnp.concatenate([hi, lo], axis=1), tri2)
        carry = carry_ref[r0:tq, 0:1]
        log_a = z - cs + carry
        if diag:
            log_a = jnp.where(before, log_a, -jnp.inf)
        a = jnp.exp(log_a)
        acc_ref[r0:tq, :] += _dot(a.astype(BF16), v)
        carry_ref[r0:tq, :] = jnp.broadcast_to(carry - cs[:, 0:1], (nr, carry_ref.shape[1]))

    acc_ref[...] = jnp.zeros_like(acc_ref)
    carry_ref[...] = jnp.zeros_like(carry_ref)
    q0 = qi * tq
    for d in range(n_diag - 1, -1, -1):
        block(pl.multiple_of(q0 + d * tk, tk), d * tk, True)

    def cond(st):
        kb, live = st
        return jnp.logical_and(kb >= 0, live > ATTN_DEAD_LOG)

    def body(st):
        kb, _ = st
        block(pl.multiple_of(kb * tk, tk), 0, False)
        return kb - 1, jnp.max(carry_ref[:, 0:1])

    lax.while_loop(cond, body, (qi * n_diag - 1, jnp.max(carry_ref[:, 0:1])))
    o_ref[...] = acc_ref[...].astype(o_ref.dtype)


def _attention(q, k, v, *, B, n_heads, tq, tk):
    BH, L, hd = q.shape
    kern = functools.partial(_attn_kernel, tq=tq, tk=tk)
    return pl.pallas_call(
        kern,
        grid=(BH, L // tq),
        in_specs=[
            pl.BlockSpec((None, tq, hd), lambda bh, i: (bh, i, 0)),
            pl.BlockSpec((None, L, hd), lambda bh, i: (bh, 0, 0)),
            pl.BlockSpec((None, L, hd), lambda bh, i: (bh, 0, 0)),
        ],
        out_specs=pl.BlockSpec((None, tq, hd), lambda bh, i: (bh // n_heads, i, bh % n_heads)),
        out_shape=jax.ShapeDtypeStruct((B, L, n_heads * hd), BF16),
        scratch_shapes=[pltpu.VMEM((tq, hd), F32), pltpu.VMEM((tq, V7X_LANES), F32)],
        compiler_params=_cparams(2),
        name="sb_attention",
    )(q, k, v)


RI_E0, RI_E1, RI_G0, RI_G1, RI_R0, RI_R1 = 0, 1, 2, 3, 4, 5


def _router_kernel(h_ref, g_ref, wh_ref, wm_ref, b_ref, ri_ref, cnt_ref, carry_ref, *, n_groups, n_experts):
    i = pl.program_id(0)
    tm = h_ref.shape[0]
    epg = n_experts // n_groups

    @pl.when(i == 0)
    def _init():
        carry_ref[...] = jnp.zeros_like(carry_ref)

    xn = _rmsnorm(h_ref[...], g_ref[...])
    xh, xm = _split2(xn)
    wh = wh_ref[...]
    logits = (_dot(xh, wh) + (_dot(xh, wm_ref[...]) + _dot(xm, wh))) + b_ref[...]

    lane = lax.broadcasted_iota(I32, (tm, V7X_LANES), 1).astype(F32)
    neg = -jnp.inf
    no_lane = float(V7X_LANES)
    gl = jnp.where(lane < n_groups, logits, neg)
    gmax = jnp.max(gl, axis=-1, keepdims=True)
    gidx = jnp.min(jnp.where(gl == gmax, lane, no_lane), axis=-1, keepdims=True)
    g_w = 1.0 / jnp.sum(jnp.exp(gl - gmax), axis=-1, keepdims=True)

    first = n_groups + gidx * epg
    in_group = jnp.logical_and(lane >= first, lane < first + epg)
    el = jnp.where(in_group, logits, neg)
    m0 = jnp.max(el, axis=-1, keepdims=True)
    i0 = jnp.min(jnp.where(el == m0, lane, no_lane), axis=-1, keepdims=True)
    el1 = jnp.where(lane == i0, neg, el)
    m1 = jnp.max(el1, axis=-1, keepdims=True)
    i1 = jnp.min(jnp.where(el1 == m1, lane, no_lane), axis=-1, keepdims=True)
    d = jnp.exp(m1 - m0)
    p0 = 1.0 / (1.0 + d)
    gate0 = g_w * p0
    gate1 = g_w * (d * p0)

    oh0 = jnp.where(lane == i0, 1.0, 0.0)
    oh1 = jnp.where(lane == i1, 1.0, 0.0)
    r = lax.broadcasted_iota(I32, (tm, tm), 0)
    cc = lax.broadcasted_iota(I32, (tm, tm), 1)
    strict = jnp.where(cc < r, 1.0, 0.0).astype(BF16)
    pre0 = _dot(strict, oh0.astype(BF16))
    pre1 = _dot(strict, oh1.astype(BF16))
    cnt0 = jnp.sum(oh0, axis=0, keepdims=True)
    cnt1 = jnp.sum(oh1, axis=0, keepdims=True)
    base = carry_ref[0:1, :]
    rank0 = jnp.sum(oh0 * (pre0 + base), axis=-1, keepdims=True)
    rank1 = jnp.sum(oh1 * (pre1 + (base + cnt0)), axis=-1, keepdims=True)
    total = base + cnt0 + cnt1
    carry_ref[...] = jnp.broadcast_to(total, carry_ref.shape)
    cnt_ref[...] = jnp.broadcast_to(total, cnt_ref.shape)

    e0 = i0 - n_groups
    e1 = i1 - n_groups
    ri = jnp.where(lane == RI_E0, e0, 0.0)
    ri = jnp.where(lane == RI_E1, e1, ri)
    ri = jnp.where(lane == RI_G0, gate0, ri)
    ri = jnp.where(lane == RI_G1, gate1, ri)
    ri = jnp.where(lane == RI_R0, rank0, ri)
    ri = jnp.where(lane == RI_R1, rank1, ri)
    ri_ref[...] = ri


def _router(h2, g, w_hi, w_mid, b_r, *, n_experts, tm):
    T, D = h2.shape
    kern = functools.partial(_router_kernel, n_groups=MOE_GROUPS, n_experts=n_experts)
    return pl.pallas_call(
        kern,
        grid=(T // tm,),
        in_specs=[
            pl.BlockSpec((tm, D), lambda i: (i, 0)),
            pl.BlockSpec((1, D), lambda i: (0, 0)),
            pl.BlockSpec((D, V7X_LANES), lambda i: (0, 0)),
            pl.BlockSpec((D, V7X_LANES), lambda i: (0, 0)),
            pl.BlockSpec((1, V7X_LANES), lambda i: (0, 0)),
        ],
        out_specs=[
            pl.BlockSpec((tm, V7X_LANES), lambda i: (i, 0)),
            pl.BlockSpec((V7X_SUBLANES, V7X_LANES), lambda i: (0, 0)),
        ],
        out_shape=[
            jax.ShapeDtypeStruct((T, V7X_LANES), F32),
            jax.ShapeDtypeStruct((V7X_SUBLANES, V7X_LANES), F32),
        ],
        scratch_shapes=[pltpu.VMEM((V7X_SUBLANES, V7X_LANES), F32)],
        compiler_params=_cparams(1),
        name="moe_router",
    )(h2, g, w_hi, w_mid, b_r)


def _invmap_kernel(dest_ref, slot_ref, *, n_rows, blk):
    i = pl.program_id(0)

    @pl.when(i == 0)
    def _fill():
        def fill(r, c):
            slot_ref[r] = -1
            return c

        lax.fori_loop(0, n_rows, fill, 0, unroll=SCALAR_UNROLL)

    def scatter(a, c):
        slot_ref[dest_ref[0, 0, a]] = i * blk + a
        return c

    lax.fori_loop(0, blk, scatter, 0, unroll=SCALAR_UNROLL)


def _invmap(dest3, n_rows):
    nb, _, blk = dest3.shape
    return pl.pallas_call(
        functools.partial(_invmap_kernel, n_rows=n_rows, blk=blk),
        grid=(nb,),
        in_specs=[pl.BlockSpec((1, 1, blk), lambda i: (i, 0, 0), memory_space=pltpu.SMEM)],
        out_specs=pl.BlockSpec(memory_space=pltpu.SMEM),
        out_shape=jax.ShapeDtypeStruct((n_rows,), I32),
        compiler_params=_cparams(1),
        name="moe_invmap",
    )(dest3)


def _tile_copy(src_ref, src_row, dst_ref, dst_row, sem):
    s = pl.multiple_of(src_row * V7X_SUBLANES, V7X_SUBLANES)
    d = pl.multiple_of(dst_row * V7X_SUBLANES, V7X_SUBLANES)
    return pltpu.make_async_copy(src_ref.at[pl.ds(s, V7X_SUBLANES)], dst_ref.at[pl.ds(d, V7X_SUBLANES)], sem)


def _expert_kernel(be_ref, tok0_ref, tokn_ref, slotp_ref, slote_ref, h8_ref, g_ref, wg_ref, wu_ref, wd_ref,
                   y8_ref, xbuf, ybuf, gsem, ssem, *, rb, n_blk):
    del be_ref
    i = pl.program_id(0)
    cur = i % 2
    nxt = 1 - cur

    def rows_loop(fn):
        def step(r, c):
            fn(r)
            return c

        lax.fori_loop(0, rb, step, 0, unroll=SCALAR_UNROLL)

    def gather(tok_ref, b):
        rows_loop(lambda r: _tile_copy(h8_ref, tok_ref[0, 0, r], xbuf.at[b], r, gsem.at[b]).start())

    def wait_gather(b):
        rows_loop(lambda r: _tile_copy(h8_ref, 0, xbuf.at[b], r, gsem.at[b]).wait())

    def scatter(slot_ref, b):
        rows_loop(lambda r: _tile_copy(ybuf.at[b], r, y8_ref, slot_ref[0, 0, r], ssem.at[b]).start())

    def wait_scatter(b):
        rows_loop(lambda r: _tile_copy(ybuf.at[b], r, y8_ref, 0, ssem.at[b]).wait())

    @pl.when(i == 0)
    def _prologue():
        ybuf[1] = jnp.zeros(ybuf.shape[1:], F32)
        gather(tok0_ref, 0)

    wait_gather(cur)
    x = _tile_rows_load(xbuf.at[cur], rb)
    xn = _rmsnorm(x, g_ref[...]).astype(BF16)
    gt = _dot(xn, wg_ref[0].astype(BF16))
    up = _dot(xn, wu_ref[0].astype(BF16))
    act = (gt * _sigmoid(gt)) * up
    y = _dot(act.astype(BF16), wd_ref[0].astype(BF16))
    gather(tokn_ref, nxt)
    scatter(slotp_ref, nxt)

    @pl.when(i >= 1)
    def _reuse():
        wait_scatter(cur)

    _tile_rows_store(ybuf.at[cur], y, rb)

    @pl.when(i == n_blk - 1)
    def _epilogue():
        scatter(slote_ref, cur)
        wait_gather(nxt)
        wait_scatter(nxt)
        wait_scatter(cur)


def _experts(blk_expert, tok3, slots3, h8, g, w_gate, w_up, w_down, *, layer, rb):
    n_blk = tok3.shape[0]
    _, E, D, F = w_gate.shape
    n_out = n_blk * rb + rb
    last = n_blk - 1

    def w_map(i, be):
        return (layer, be[i], 0, 0)

    smem = lambda imap: pl.BlockSpec((1, 1, rb), imap, memory_space=pltpu.SMEM)
    grid_spec = pltpu.PrefetchScalarGridSpec(
        num_scalar_prefetch=1,
        grid=(n_blk,),
        in_specs=[
            smem(lambda i, be: (0, 0, 0)),
            smem(lambda i, be: (jnp.minimum(i + 1, last), 0, 0)),
            smem(lambda i, be: (i, 0, 0)),
            smem(lambda i, be: (jnp.minimum(i + 1, n_blk), 0, 0)),
            pl.BlockSpec(memory_space=pl.ANY),
            pl.BlockSpec((1, D), lambda i, be: (0, 0)),
            pl.BlockSpec((None, 1, D, F), w_map),
            pl.BlockSpec((None, 1, D, F), w_map),
            pl.BlockSpec((None, 1, F, D), w_map),
        ],
        out_specs=pl.BlockSpec(memory_space=pl.ANY),
        scratch_shapes=[
            pltpu.VMEM((2, rb * V7X_SUBLANES, V7X_LANES), F32),
            pltpu.VMEM((2, rb * V7X_SUBLANES, V7X_LANES), F32),
            pltpu.SemaphoreType.DMA((2,)),
            pltpu.SemaphoreType.DMA((2,)),
        ],
    )
    return pl.pallas_call(
        functools.partial(_expert_kernel, rb=rb, n_blk=n_blk),
        grid_spec=grid_spec,
        out_shape=jax.ShapeDtypeStruct((n_out * V7X_SUBLANES, V7X_LANES), F32),
        compiler_params=_cparams(1),
        name="moe_experts",
    )(blk_expert, tok3, tok3, slots3, slots3, h8, g, w_gate, w_up, w_down)


def _combine_kernel(h_ref, ri_ref, p_ref, y0_ref, y1_ref, pn_ref, wg_ref, bg_ref, wp_ref, fn_ref, o_ref,
                    *, final_norm):
    tm = h_ref.shape[0]
    ri = ri_ref[...]
    h = h_ref[...] + (_tile_rows_load(y0_ref, tm) * ri[:, RI_G0:RI_G0 + 1]
                      + _tile_rows_load(y1_ref, tm) * ri[:, RI_G1:RI_G1 + 1])
    xn = _rmsnorm(h, pn_ref[...]).astype(BF16)
    gate = _sigmoid(_dot(xn, wg_ref[...]) + bg_ref[...])
    h = h + gate * _dot(p_ref[...].astype(BF16), wp_ref[...])
    if final_norm:
        h = _rmsnorm(h, fn_ref[...])
    o_ref[...] = h


def _combine(h2, ri, p3, y8, pn, wg, bg, wp, fnorm, *, layer, tm, final_norm):
    T, D = h2.shape
    P = p3.shape[2]
    kern = functools.partial(_combine_kernel, final_norm=final_norm)
    row = lambda i: (0, 0)
    k1 = T // tm
    return pl.pallas_call(
        kern,
        grid=(T // tm,),
        in_specs=[
            pl.BlockSpec((tm, D), lambda i: (i, 0)),
            pl.BlockSpec((tm, V7X_LANES), lambda i: (i, 0)),
            pl.BlockSpec((None, tm, P), lambda i: (layer, i, 0)),
            pl.BlockSpec((tm * V7X_SUBLANES, V7X_LANES), lambda i: (i, 0)),
            pl.BlockSpec((tm * V7X_SUBLANES, V7X_LANES), lambda i: (k1 + i, 0)),
            pl.BlockSpec((1, D), row),
            pl.BlockSpec((D, D), row),
            pl.BlockSpec((1, D), row),
            pl.BlockSpec((P, D), row),
            pl.BlockSpec((1, D), row),
        ],
        out_specs=pl.BlockSpec((tm, D), lambda i: (i, 0)),
        out_shape=jax.ShapeDtypeStruct((T, D), F32),
        compiler_params=_cparams(1),
        name="moe_combine_ple",
    )(h2, ri, p3, y8, y8, pn, wg, bg, wp, fnorm)


TM_PROJ = 256
TM_RES = 512
TM_TOKEN = 256
EXPERT_ROWS = 256
INVMAP_BLOCK = 2048
ATTN_Q_BLOCK = 512
ATTN_K_BLOCK = 256


def _row(v):
    return v.reshape(1, -1).astype(F32)


def _moe_ple(h2, p3, moe_norm, w_rg, b_rg, w_re, b_re, w_gate, w_up, w_down,
             ple_norm, ple_w_gate, ple_b_gate, ple_w_proj, final_norm, *, layer, last):
    T, D = h2.shape
    G = w_rg.shape[1]
    E = w_re.shape[1]
    TK = T * MOE_TOP_K
    pad = V7X_LANES - G - E
    w_r = jnp.concatenate([w_rg, w_re, jnp.zeros((D, pad), F32)], axis=1)
    b_r = jnp.concatenate([b_rg, b_re, jnp.zeros((pad,), F32)]).reshape(1, V7X_LANES)
    w_hi = w_r.astype(BF16)
    w_mid = (w_r - w_hi.astype(F32)).astype(BF16)
    ri, cnt = _router(h2, _row(moe_norm), w_hi, w_mid, b_r, n_experts=E, tm=TM_TOKEN)

    rb = EXPERT_ROWS
    counts = cnt[0, G:G + E].astype(I32)
    padded = ((counts + rb - 1) // rb) * rb
    pad_end = jnp.cumsum(padded)
    pad_start = pad_end - padded
    cum_incl = jnp.cumsum(counts)
    n_blk = TK // rb + E
    n_rows = n_blk * rb
    eids = jnp.arange(E, dtype=I32)
    expert = ri[:, RI_E0:RI_E1 + 1].astype(I32).T
    rank = ri[:, RI_R0:RI_R1 + 1].astype(I32).T
    dest = jnp.sum(jnp.where(expert[:, :, None] == eids, pad_start, 0), axis=-1) + rank
    slot = _invmap(dest.reshape(TK // INVMAP_BLOCK, 1, INVMAP_BLOCK), n_rows)

    blk_row0 = jnp.arange(n_blk, dtype=I32) * rb
    blk_expert = jnp.minimum(jnp.sum((pad_end[None, :] <= blk_row0[:, None]).astype(I32), axis=1), E - 1)
    blk_valid = jnp.sum(jnp.where(blk_expert[:, None] == eids, cum_incl, 0), axis=-1)
    rows = jnp.arange(n_rows, dtype=I32)
    spare = TK + rows - jnp.repeat(blk_valid, rb)
    is_pad = slot < 0
    slot_full = jnp.where(is_pad, spare, slot)
    row_tok = jnp.where(is_pad, 0, jnp.where(slot >= T, slot - T, slot))
    slots3 = jnp.concatenate([n_rows + jnp.arange(rb, dtype=I32), slot_full]).reshape(n_blk + 1, 1, rb)
    tok3 = row_tok.reshape(n_blk, 1, rb)

    h8 = h2.reshape(T * V7X_SUBLANES, D // V7X_SUBLANES)
    y8 = _experts(blk_expert, tok3, slots3, h8, _row(moe_norm), w_gate, w_up, w_down, layer=layer, rb=rb)
    return _combine(h2, ri, p3, y8, _row(ple_norm), ple_w_gate.astype(BF16), _row(ple_b_gate),
                    ple_w_proj.astype(BF16), _row(final_norm), layer=layer, tm=TM_TOKEN, final_norm=last)


def _ssd_layer(h2, B, L, norm, w_in, conv_w, conv_b, dt_bias, a_log, d_skip, gnorm, w_out):
    T, D = h2.shape
    H = a_log.shape[0]
    d_inner = H * SSD_HEAD_DIM
    cdim = conv_w.shape[1]
    wz = w_in[:, :d_inner].astype(BF16)
    wx = w_in[:, d_inner:d_inner + cdim].astype(BF16)
    padh = V7X_LANES - H
    wdt = jnp.concatenate([w_in[:, d_inner + cdim:], jnp.zeros((D, padh), F32)], axis=1).astype(BF16)
    dtb = jnp.concatenate([dt_bias, jnp.zeros((padh,), F32)]).reshape(1, V7X_LANES)
    a_row = jnp.concatenate([-jnp.exp(a_log.astype(F32)), jnp.zeros((padh,), F32)]).reshape(1, V7X_LANES)
    dsk = jnp.repeat(d_skip.astype(F32), SSD_HEAD_DIM).reshape(1, d_inner)

    z, xbc, dt = _in_proj(h2, _row(norm), wz, wx, wdt, dtb, tm=TM_PROJ)
    yn = _ssd_scan(xbc.reshape(B, L, cdim), z.reshape(B, L, d_inner), dt.reshape(B, L, V7X_LANES),
                   conv_w.astype(F32), _row(conv_b), a_row, dsk, _row(gnorm), d_inner=d_inner)
    return _proj_res(yn.reshape(T, d_inner), w_out.astype(BF16), h2, tm=TM_RES)


def _sb_layer(h2, B, L, norm, w_qkv, w_o):
    T, D = h2.shape
    hd = SB_HEAD_DIM
    n_heads = w_o.shape[0] // hd
    q, k, v = _qkv_proj(h2.reshape(B, L, D), _row(norm), w_qkv.astype(BF16),
                        n_heads=n_heads, hd=hd, tm=TM_PROJ)
    bh = B * n_heads
    o = _attention(q.reshape(bh, L, hd), k.reshape(bh, L, hd), v.reshape(bh, L, hd),
                   B=B, n_heads=n_heads, tq=min(ATTN_Q_BLOCK, L), tk=ATTN_K_BLOCK)
    return _proj_res(o.reshape(T, n_heads * hd), w_o.astype(BF16), h2, tm=TM_RES)


def kernel(x, p, ssd_norm, ssd_w_in, ssd_conv_w, ssd_conv_b, ssd_dt_bias, ssd_a_log, ssd_d, ssd_gnorm, ssd_w_out, sb_norm, sb_w_qkv, sb_w_o, moe_norm, moe_w_rg, moe_b_rg, moe_w_re, moe_b_re, moe_w_gate, moe_w_up, moe_w_down, ple_norm, ple_w_gate, ple_b_gate, ple_w_proj, final_norm):
    B, L, D = x.shape
    depth = p.shape[0]
    T = B * L
    n_mixers = 2
    h = x.reshape(T, D)
    p3 = p.reshape(depth, T, p.shape[-1])
    for i in range(depth):
        j = i // n_mixers
        if i % n_mixers == 0:
            h = _ssd_layer(h, B, L, ssd_norm[j], ssd_w_in[j], ssd_conv_w[j], ssd_conv_b[j], ssd_dt_bias[j],
                           ssd_a_log[j], ssd_d[j], ssd_gnorm[j], ssd_w_out[j])
        else:
            h = _sb_layer(h, B, L, sb_norm[j], sb_w_qkv[j], sb_w_o[j])
        h = _moe_ple(h, p3, moe_norm[i], moe_w_rg[i], moe_b_rg[i], moe_w_re[i], moe_b_re[i],
                     moe_w_gate, moe_w_up, moe_w_down, ple_norm[i], ple_w_gate[i], ple_b_gate[i],
                     ple_w_proj[i], final_norm, layer=i, last=(i == depth - 1))
    return h.reshape(B, L, D)
```

```python
import functools
import math

import jax
import jax.numpy as jnp
from jax import lax
from jax.experimental import pallas as pl
from jax.experimental.pallas import tpu as pltpu

F32 = jnp.float32
BF16 = jnp.bfloat16
I32 = jnp.int32

EPS = 1e-6
V7X_LANES = 128
V7X_SUBLANES = 8
V7X_VMEM_LIMIT_BYTES = 56 * 1024 * 1024

SSD_HEAD_DIM = 64
SSD_N_GROUPS = 4
SSD_D_STATE = 128
SSD_CHUNK = 128
SSD_CONV = 4
SB_HEAD_DIM = 128
MOE_GROUPS = 8
MOE_TOP_K = 2

SCALAR_UNROLL = 8

ATTN_DEAD_LOG = -110.0


def _cparams(n_axes):
    return pltpu.CompilerParams(
        dimension_semantics=("arbitrary",) * n_axes,
        vmem_limit_bytes=V7X_VMEM_LIMIT_BYTES,
    )


def _dot(a, b):
    return jnp.dot(a, b, preferred_element_type=F32)


def _dot_nt(a, b):
    return lax.dot_general(a, b, (((1,), (1,)), ((), ())), preferred_element_type=F32)


def _split3(x):
    hi = x.astype(BF16)
    r1 = x - hi.astype(F32)
    mid = r1.astype(BF16)
    lo = (r1 - mid.astype(F32)).astype(BF16)
    return hi, mid, lo


def _split2(x):
    hi = x.astype(BF16)
    lo = (x - hi.astype(F32)).astype(BF16)
    return hi, lo


def _dot_exact_lhs(a_bf16, x):
    hi, mid, lo = _split3(x)
    return _dot(a_bf16, hi) + _dot(a_bf16, mid) + _dot(a_bf16, lo)


def _sigmoid(x):
    return 1.0 / (1.0 + jnp.exp(-x))


def _softplus(x):
    return jnp.maximum(x, 0.0) + jnp.log(1.0 + jnp.exp(-jnp.abs(x)))


def _rmsnorm(x, g):
    ms = jnp.mean(x * x, axis=-1, keepdims=True)
    return (x * lax.rsqrt(ms + EPS)) * g


def _tile_rows_load(ref, n):
    return jnp.concatenate(
        [ref[pl.ds(j, n, stride=V7X_SUBLANES), :] for j in range(V7X_SUBLANES)], axis=1)


def _tile_rows_store(ref, val, n):
    for j in range(V7X_SUBLANES):
        ref[pl.ds(j, n, stride=V7X_SUBLANES), :] = val[:, j * V7X_LANES:(j + 1) * V7X_LANES]


def _in_proj_kernel(x_ref, g_ref, wz_ref, wx_ref, wdt_ref, dtb_ref, z_ref, xbc_ref, dt_ref):
    xn = _rmsnorm(x_ref[...], g_ref[...]).astype(BF16)
    z_ref[...] = _dot(xn, wz_ref[...])
    xbc_ref[...] = _dot(xn, wx_ref[...])
    dt_ref[...] = _softplus(_dot(xn, wdt_ref[...]) + dtb_ref[...])


def _in_proj(x2, g, wz, wx, wdt, dtb, *, tm):
    T, D = x2.shape
    nz, nx, nd = wz.shape[1], wx.shape[1], wdt.shape[1]
    full = lambda i: (0, 0)
    return pl.pallas_call(
        _in_proj_kernel,
        grid=(T // tm,),
        in_specs=[
            pl.BlockSpec((tm, D), lambda i: (i, 0)),
            pl.BlockSpec((1, D), full),
            pl.BlockSpec((D, nz), full),
            pl.BlockSpec((D, nx), full),
            pl.BlockSpec((D, nd), full),
            pl.BlockSpec((1, nd), full),
        ],
        out_specs=[
            pl.BlockSpec((tm, nz), lambda i: (i, 0)),
            pl.BlockSpec((tm, nx), lambda i: (i, 0)),
            pl.BlockSpec((tm, nd), lambda i: (i, 0)),
        ],
        out_shape=[
            jax.ShapeDtypeStruct((T, nz), F32),
            jax.ShapeDtypeStruct((T, nx), F32),
            jax.ShapeDtypeStruct((T, nd), F32),
        ],
        compiler_params=_cparams(1),
        name="ssd_in_proj",
    )(x2, g, wz, wx, wdt, dtb)


def _ssd_kernel(xbc_ref, z_ref, dt_ref, cw_ref, cb_ref, a_ref, dsk_ref, gn_ref, yn_ref,
                state_ref, cbuf_ref, act_ref, yz_ref, *, Q, d_inner, n_groups, d_state, head_dim):
    c = pl.program_id(1)
    N = d_state
    gw = d_inner // n_groups
    pairs_per_group = gw // V7X_LANES
    halo = V7X_SUBLANES

    @pl.when(c == 0)
    def _init():
        state_ref[...] = jnp.zeros_like(state_ref)
        cbuf_ref[0:halo, :] = jnp.zeros((halo, cbuf_ref.shape[1]), F32)

    x_in = xbc_ref[...]
    cbuf_ref[halo:halo + Q, :] = x_in
    cw = cw_ref[...]
    conv = x_in * cw[SSD_CONV - 1:SSD_CONV, :] + cb_ref[...]
    for k in range(SSD_CONV - 1):
        off = halo - (SSD_CONV - 1) + k
        conv = conv + cbuf_ref[off:off + Q, :] * cw[k:k + 1, :]
    cbuf_ref[0:halo, :] = x_in[Q - halo:Q, :]
    act_ref[...] = conv * _sigmoid(conv)

    dt = dt_ref[...]
    rows = lax.broadcasted_iota(I32, (Q, Q), 0)
    cols = lax.broadcasted_iota(I32, (Q, Q), 1)
    causal = rows >= cols
    tril = jnp.where(causal, 1.0, 0.0).astype(BF16)
    acum = _dot_exact_lhs(tril, dt * a_ref[...])
    acum_t = acum.T
    a_last = acum[Q - 1:Q, :]
    dte = jnp.exp(a_last - acum)
    din = jnp.exp(acum)
    cdec = jnp.exp(a_last)

    lane = lax.broadcasted_iota(I32, (Q, V7X_LANES), 1)
    lo_half = lane < head_dim
    lo_half_row = lo_half[0:1, :]

    def pair(v, h0, mask):
        return jnp.where(mask, v[:, h0:h0 + 1], v[:, h0 + 1:h0 + 2])

    for g in range(n_groups):
        b_f = act_ref[:, d_inner + g * N:d_inner + (g + 1) * N]
        b_g = b_f.astype(BF16)
        c_g = act_ref[:, d_inner + n_groups * N + g * N:d_inner + n_groups * N + (g + 1) * N].astype(BF16)
        cb = _dot_nt(c_g, b_g)
        s_old = state_ref[:, g * gw:(g + 1) * gw]
        y_off = _dot(c_g, s_old.astype(BF16))
        xw_parts = []
        cd_parts = []
        for jj in range(pairs_per_group):
            j = g * pairs_per_group + jj
            h0 = 2 * j
            sl = slice(j * V7X_LANES, (j + 1) * V7X_LANES)
            xs_p = act_ref[:, sl]
            xdt = xs_p * pair(dt, h0, lo_half)
            xdt_b = xdt.astype(BF16)
            ys = []
            for hh in (h0, h0 + 1):
                seg = acum[:, hh:hh + 1] - acum_t[hh:hh + 1, :]
                dec = jnp.exp(jnp.where(causal, seg, -jnp.inf))
                ys.append(_dot((cb * dec).astype(BF16), xdt_b))
            y = jnp.where(lo_half, ys[0], ys[1])
            y = y + y_off[:, jj * V7X_LANES:(jj + 1) * V7X_LANES] * pair(din, h0, lo_half)
            y = y + xs_p * dsk_ref[:, sl]
            zp = z_ref[:, sl]
            yz_ref[:, sl] = y * (zp * _sigmoid(zp))
            xw_parts.append((xdt * pair(dte, h0, lo_half)).astype(BF16))
            cd_parts.append(pair(cdec, h0, lo_half_row))
        xw_g = jnp.concatenate(xw_parts, axis=1)
        new_states = _dot(b_f.T.astype(BF16), xw_g)
        state_ref[:, g * gw:(g + 1) * gw] = s_old * jnp.concatenate(cd_parts, axis=1) + new_states
        yzg = yz_ref[:, g * gw:(g + 1) * gw]
        ms = jnp.mean(yzg * yzg, axis=-1, keepdims=True)
        yn_ref[:, g * gw:(g + 1) * gw] = (
            (yzg * lax.rsqrt(ms + EPS)) * gn_ref[:, g * gw:(g + 1) * gw]).astype(yn_ref.dtype)


def _ssd_scan(xbc, z, dt, cw, cb, a_row, dsk, gn, *, d_inner):
    B, L, cdim = xbc.shape
    Q = SSD_CHUNK
    nd = dt.shape[-1]
    full2 = lambda b, c: (0, 0)
    kern = functools.partial(_ssd_kernel, Q=Q, d_inner=d_inner, n_groups=SSD_N_GROUPS,
                             d_state=SSD_D_STATE, head_dim=SSD_HEAD_DIM)
    return pl.pallas_call(
        kern,
        grid=(B, L // Q),
        in_specs=[
            pl.BlockSpec((None, Q, cdim), lambda b, c: (b, c, 0)),
            pl.BlockSpec((None, Q, d_inner), lambda b, c: (b, c, 0)),
            pl.BlockSpec((None, Q, nd), lambda b, c: (b, c, 0)),
            pl.BlockSpec(cw.shape, full2),
            pl.BlockSpec(cb.shape, full2),
            pl.BlockSpec(a_row.shape, full2),
            pl.BlockSpec(dsk.shape, full2),
            pl.BlockSpec(gn.shape, full2),
        ],
        out_specs=pl.BlockSpec((None, Q, d_inner), lambda b, c: (b, c, 0)),
        out_shape=jax.ShapeDtypeStruct((B, L, d_inner), BF16),
        scratch_shapes=[
            pltpu.VMEM((SSD_D_STATE, d_inner), F32),
            pltpu.VMEM((Q + V7X_SUBLANES, cdim), F32),
            pltpu.VMEM((Q, cdim), F32),
            pltpu.VMEM((Q, d_inner), F32),
        ],
        compiler_params=_cparams(2),
        name="ssd_scan",
    )(xbc, z, dt, cw, cb, a_row, dsk, gn)


def _proj_res_kernel(a_ref, w_ref, r_ref, o_ref):
    o_ref[...] = r_ref[...] + _dot(a_ref[...], w_ref[...])


def _proj_res(a, w, res, *, tm):
    T, K = a.shape
    N = w.shape[1]
    return pl.pallas_call(
        _proj_res_kernel,
        grid=(T // tm,),
        in_specs=[
            pl.BlockSpec((tm, K), lambda i: (i, 0)),
            pl.BlockSpec((K, N), lambda i: (0, 0)),
            pl.BlockSpec((tm, N), lambda i: (i, 0)),
        ],
        out_specs=pl.BlockSpec((tm, N), lambda i: (i, 0)),
        out_shape=jax.ShapeDtypeStruct((T, N), F32),
        compiler_params=_cparams(1),
        name="proj_residual",
    )(a, w, res)


def _qkv_kernel(x_ref, g_ref, w_ref, q_ref, k_ref, v_ref, *, n_heads, hd, scale):
    xn = _rmsnorm(x_ref[...], g_ref[...]).astype(BF16)
    qkv = _dot(xn, w_ref[...])
    d = n_heads * hd
    for h in range(n_heads):
        q_ref[h] = (qkv[:, h * hd:(h + 1) * hd] * scale).astype(BF16)
        k_ref[h] = qkv[:, d + h * hd:d + (h + 1) * hd].astype(BF16)
        v_ref[h] = qkv[:, 2 * d + h * hd:2 * d + (h + 1) * hd].astype(BF16)


def _qkv_proj(x, g, w, *, n_heads, hd, tm):
    B, L, D = x.shape
    kern = functools.partial(_qkv_kernel, n_heads=n_heads, hd=hd, scale=1.0 / math.sqrt(hd))
    hspec = pl.BlockSpec((None, n_heads, tm, hd), lambda b, i: (b, 0, i, 0))
    hshape = jax.ShapeDtypeStruct((B, n_heads, L, hd), BF16)
    return pl.pallas_call(
        kern,
        grid=(B, L // tm),
        in_specs=[
            pl.BlockSpec((None, tm, D), lambda b, i: (b, i, 0)),
            pl.BlockSpec((1, D), lambda b, i: (0, 0)),
            pl.BlockSpec(w.shape, lambda b, i: (0, 0)),
        ],
        out_specs=[hspec, hspec, hspec],
        out_shape=[hshape, hshape, hshape],
        compiler_params=_cparams(2),
        name="sb_qkv_proj",
    )(x, g, w)


def _attn_kernel(q_ref, k_ref, v_ref, o_ref, acc_ref, carry_ref, *, tq, tk):
    qi = pl.program_id(1)
    n_diag = tq // tk
    rows_t = lax.broadcasted_iota(I32, (tk, tk), 0)
    cols_t = lax.broadcasted_iota(I32, (tk, tk), 1)
    tri = jnp.where(rows_t >= cols_t, 1.0, 0.0).astype(BF16)
    tri2 = jnp.concatenate([tri, tri], axis=0)

    def block(k0, r0, diag):
        nr = tq - r0
        q = q_ref[r0:tq, :]
        k = k_ref[pl.ds(k0, tk), :]
        v = v_ref[pl.ds(k0, tk), :]
        z = _dot_nt(q, k)
        sp = _softplus(z)
        if diag:
            before = (lax.broadcasted_iota(I32, (nr, tk), 1) < lax.broadcasted_iota(I32, (nr, tk), 0))
            sp = jnp.where(before, sp, 0.0)
        hi, lo = _split2(sp)
        cs = _dot(jnp.concatenate([hi, lo], axis=1), tri2)
        carry = carry_ref[r0:tq, 0:1]
        log_a = z - cs + carry
        if diag:
            log_a = jnp.where(before, log_a, -jnp.inf)
        a = jnp.exp(log_a)
        acc_ref[r0:tq, :] += _dot(a.astype(BF16), v)
        carry_ref[r0:tq, :] = jnp.broadcast_to(carry - cs[:, 0:1], (nr, carry_ref.shape[1]))

    acc_ref[...] = jnp.zeros_like(acc_ref)
    carry_ref[...] = jnp.zeros_like(carry_ref)
    q0 = qi * tq
    for d in range(n_diag - 1, -1, -1):
        block(pl.multiple_of(q0 + d * tk, tk), d * tk, True)

    def cond(st):
        kb, live = st
        return jnp.logical_and(kb >= 0, live > ATTN_DEAD_LOG)

    def body(st):
        kb, _ = st
        block(pl.multiple_of(kb * tk, tk), 0, False)
        return kb - 1, jnp.max(carry_ref[:, 0:1])

    lax.while_loop(cond, body, (qi * n_diag - 1, jnp.max(carry_ref[:, 0:1])))
    o_ref[...] = acc_ref[...].astype(o_ref.dtype)


def _attention(q, k, v, *, B, n_heads, tq, tk):
    BH, L, hd = q.shape
    kern = functools.partial(_attn_kernel, tq=tq, tk=tk)
    return pl.pallas_call(
        kern,
        grid=(BH, L // tq),
        in_specs=[
            pl.BlockSpec((None, tq, hd), lambda bh, i: (bh, i, 0)),
            pl.BlockSpec((None, L, hd), lambda bh, i: (bh, 0, 0)),
            pl.BlockSpec((None, L, hd), lambda bh, i: (bh, 0, 0)),
        ],
        out_specs=pl.BlockSpec((None, tq, hd), lambda bh, i: (bh // n_heads, i, bh % n_heads)),
        out_shape=jax.ShapeDtypeStruct((B, L, n_heads * hd), BF16),
        scratch_shapes=[pltpu.VMEM((tq, hd), F32), pltpu.VMEM((tq, V7X_LANES), F32)],
        compiler_params=_cparams(2),
        name="sb_attention",
    )(q, k, v)


RI_E0, RI_E1, RI_G0, RI_G1, RI_R0, RI_R1 = 0, 1, 2, 3, 4, 5


def _router_kernel(h_ref, g_ref, wh_ref, wm_ref, b_ref, ri_ref, cnt_ref, carry_ref, *, n_groups, n_experts):
    i = pl.program_id(0)
    tm = h_ref.shape[0]
    epg = n_experts // n_groups

    @pl.when(i == 0)
    def _init():
        carry_ref[...] = jnp.zeros_like(carry_ref)

    xn = _rmsnorm(h_ref[...], g_ref[...])
    xh, xm = _split2(xn)
    wh = wh_ref[...]
    logits = (_dot(xh, wh) + (_dot(xh, wm_ref[...]) + _dot(xm, wh))) + b_ref[...]

    lane = lax.broadcasted_iota(I32, (tm, V7X_LANES), 1).astype(F32)
    neg = -jnp.inf
    no_lane = float(V7X_LANES)
    gl = jnp.where(lane < n_groups, logits, neg)
    gmax = jnp.max(gl, axis=-1, keepdims=True)
    gidx = jnp.min(jnp.where(gl == gmax, lane, no_lane), axis=-1, keepdims=True)
    g_w = 1.0 / jnp.sum(jnp.exp(gl - gmax), axis=-1, keepdims=True)

    first = n_groups + gidx * epg
    in_group = jnp.logical_and(lane >= first, lane < first + epg)
    el = jnp.where(in_group, logits, neg)
    m0 = jnp.max(el, axis=-1, keepdims=True)
    i0 = jnp.min(jnp.where(el == m0, lane, no_lane), axis=-1, keepdims=True)
    el1 = jnp.where(lane == i0, neg, el)
    m1 = jnp.max(el1, axis=-1, keepdims=True)
    i1 = jnp.min(jnp.where(el1 == m1, lane, no_lane), axis=-1, keepdims=True)
    d = jnp.exp(m1 - m0)
    p0 = 1.0 / (1.0 + d)
    gate0 = g_w * p0
    gate1 = g_w * (d * p0)

    oh0 = jnp.where(lane == i0, 1.0, 0.0)
    oh1 = jnp.where(lane == i1, 1.0, 0.0)
    r = lax.broadcasted_iota(I32, (tm, tm), 0)
    cc = lax.broadcasted_iota(I32, (tm, tm), 1)
    strict = jnp.where(cc < r, 1.0, 0.0).astype(BF16)
    pre0 = _dot(strict, oh0.astype(BF16))
    pre1 = _dot(strict, oh1.astype(BF16))
    cnt0 = jnp.sum(oh0, axis=0, keepdims=True)
    cnt1 = jnp.sum(oh1, axis=0, keepdims=True)
    base = carry_ref[0:1, :]
    rank0 = jnp.sum(oh0 * (pre0 + base), axis=-1, keepdims=True)
    rank1 = jnp.sum(oh1 * (pre1 + (base + cnt0)), axis=-1, keepdims=True)
    total = base + cnt0 + cnt1
    carry_ref[...] = jnp.broadcast_to(total, carry_ref.shape)
    cnt_ref[...] = jnp.broadcast_to(total, cnt_ref.shape)

    e0 = i0 - n_groups
    e1 = i1 - n_groups
    ri = jnp.where(lane == RI_E0, e0, 0.0)
    ri = jnp.where(lane == RI_E1, e1, ri)
    ri = jnp.where(lane == RI_G0, gate0, ri)
    ri = jnp.where(lane == RI_G1, gate1, ri)
    ri = jnp.where(lane == RI_R0, rank0, ri)
    ri = jnp.where(lane == RI_R1, rank1, ri)
    ri_ref[...] = ri


def _router(h2, g, w_hi, w_mid, b_r, *, n_experts, tm):
    T, D = h2.shape
    kern = functools.partial(_router_kernel, n_groups=MOE_GROUPS, n_experts=n_experts)
    return pl.pallas_call(
        kern,
        grid=(T // tm,),
        in_specs=[
            pl.BlockSpec((tm, D), lambda i: (i, 0)),
            pl.BlockSpec((1, D), lambda i: (0, 0)),
            pl.BlockSpec((D, V7X_LANES), lambda i: (0, 0)),
            pl.BlockSpec((D, V7X_LANES), lambda i: (0, 0)),
            pl.BlockSpec((1, V7X_LANES), lambda i: (0, 0)),
        ],
        out_specs=[
            pl.BlockSpec((tm, V7X_LANES), lambda i: (i, 0)),
            pl.BlockSpec((V7X_SUBLANES, V7X_LANES), lambda i: (0, 0)),
        ],
        out_shape=[
            jax.ShapeDtypeStruct((T, V7X_LANES), F32),
            jax.ShapeDtypeStruct((V7X_SUBLANES, V7X_LANES), F32),
        ],
        scratch_shapes=[pltpu.VMEM((V7X_SUBLANES, V7X_LANES), F32)],
        compiler_params=_cparams(1),
        name="moe_router",
    )(h2, g, w_hi, w_mid, b_r)


def _tile_copy(src_ref, src_row, dst_ref, dst_row, sem):
    s = pl.multiple_of(src_row * V7X_SUBLANES, V7X_SUBLANES)
    d = pl.multiple_of(dst_row * V7X_SUBLANES, V7X_SUBLANES)
    return pltpu.make_async_copy(src_ref.at[pl.ds(s, V7X_SUBLANES)], dst_ref.at[pl.ds(d, V7X_SUBLANES)], sem)


def _token_loop(tm, fn):
    def step(t, c):
        fn(t)
        return c

    lax.fori_loop(0, tm, step, 0, unroll=SCALAR_UNROLL)


def _dispatch_kernel(dest_ref, h_ref, g_ref, init_ref, rows_ref, xt_ref, sem):
    del init_ref
    i = pl.program_id(0)
    tm = h_ref.shape[0]
    cur = i % 2
    _tile_rows_store(xt_ref.at[cur], _rmsnorm(h_ref[...], g_ref[...]), tm)

    def issue(t):
        for k in range(MOE_TOP_K):
            _tile_copy(xt_ref.at[cur], t, rows_ref, dest_ref[0, 0, MOE_TOP_K * t + k],
                       sem.at[cur]).start(priority=k)

    def drain(b):
        def wait(t):
            for _ in range(MOE_TOP_K):
                _tile_copy(xt_ref.at[b], t, rows_ref, 0, sem.at[b]).wait()

        _token_loop(tm, wait)

    _token_loop(tm, issue)

    @pl.when(i >= 1)
    def _prev():
        drain(1 - cur)

    @pl.when(i == pl.num_programs(0) - 1)
    def _last():
        drain(cur)


def _dispatch(dest3, h2, g, n_rows, *, tm):
    T, D = h2.shape
    tiles = (n_rows * V7X_SUBLANES, V7X_LANES)
    init = jnp.zeros(tiles, F32)
    return pl.pallas_call(
        _dispatch_kernel,
        grid=(T // tm,),
        in_specs=[
            pl.BlockSpec((1, 1, MOE_TOP_K * tm), lambda i: (i, 0, 0), memory_space=pltpu.SMEM),
            pl.BlockSpec((tm, D), lambda i: (i, 0)),
            pl.BlockSpec((1, D), lambda i: (0, 0)),
            pl.BlockSpec(memory_space=pl.ANY),
        ],
        out_specs=pl.BlockSpec(memory_space=pl.ANY),
        out_shape=jax.ShapeDtypeStruct(tiles, F32),
        scratch_shapes=[pltpu.VMEM((2, tm * V7X_SUBLANES, V7X_LANES), F32), pltpu.SemaphoreType.DMA((2,))],
        input_output_aliases={3: 0},
        compiler_params=_cparams(1),
        name="moe_dispatch",
    )(dest3, h2, g, init)


def _expert_kernel(be_ref, nu_ref, x_ref, wg_ref, wu_ref, wd_ref, y_ref):
    i = pl.program_id(0)
    rb = x_ref.shape[0] // V7X_SUBLANES

    @pl.when(i < nu_ref[0])
    def _live():
        x = _tile_rows_load(x_ref, rb).astype(BF16)
        gt = _dot(x, wg_ref[0].astype(BF16))
        up = _dot(x, wu_ref[0].astype(BF16))
        act = (gt * _sigmoid(gt)) * up
        _tile_rows_store(y_ref, _dot(act.astype(BF16), wd_ref[0].astype(BF16)), rb)

    @pl.when(i >= nu_ref[0])
    def _dead():
        y_ref[...] = jnp.zeros_like(y_ref)


def _experts(blk_expert, n_used, x_rows, w_gate, w_up, w_down, *, layer, rb):
    n_blk = x_rows.shape[0] // (rb * V7X_SUBLANES)
    _, E, D, F = w_gate.shape
    tile_blk = (rb * V7X_SUBLANES, V7X_LANES)

    def row_map(i, be, nu):
        return (jnp.minimum(i, nu[0] - 1), 0)

    def w_map(i, be, nu):
        return (layer, be[i], 0, 0)

    grid_spec = pltpu.PrefetchScalarGridSpec(
        num_scalar_prefetch=2,
        grid=(n_blk,),
        in_specs=[
            pl.BlockSpec(tile_blk, row_map),
            pl.BlockSpec((None, 1, D, F), w_map),
            pl.BlockSpec((None, 1, D, F), w_map),
            pl.BlockSpec((None, 1, F, D), w_map),
        ],
        out_specs=pl.BlockSpec(tile_blk, lambda i, be, nu: (i, 0)),
    )
    return pl.pallas_call(
        _expert_kernel,
        grid_spec=grid_spec,
        out_shape=jax.ShapeDtypeStruct(x_rows.shape, F32),
        compiler_params=_cparams(1),
        name="moe_experts",
    )(blk_expert, n_used, x_rows, w_gate, w_up, w_down)


def _combine_kernel(dest_ref, destn_ref, h_ref, ri_ref, p_ref, yrows_ref, pn_ref, wg_ref, bg_ref, wp_ref,
                    fn_ref, o_ref, ybuf_ref, sem, *, final_norm):
    i = pl.program_id(0)
    tm = h_ref.shape[0]
    cur = i % 2

    def fetch(d_ref, b):
        def issue(t):
            for k in range(MOE_TOP_K):
                _tile_copy(yrows_ref, d_ref[0, 0, MOE_TOP_K * t + k], ybuf_ref.at[b, k], t,
                           sem.at[b]).start(priority=k)

        _token_loop(tm, issue)

    @pl.when(i == 0)
    def _first():
        fetch(dest_ref, 0)

    @pl.when(i + 1 < pl.num_programs(0))
    def _next():
        fetch(destn_ref, 1 - cur)

    def wait(t):
        for k in range(MOE_TOP_K):
            _tile_copy(yrows_ref, 0, ybuf_ref.at[cur, k], t, sem.at[cur]).wait()

    _token_loop(tm, wait)

    ri = ri_ref[...]
    h = h_ref[...] + (_tile_rows_load(ybuf_ref.at[cur, 0], tm) * ri[:, RI_G0:RI_G0 + 1]
                      + _tile_rows_load(ybuf_ref.at[cur, 1], tm) * ri[:, RI_G1:RI_G1 + 1])
    xn = _rmsnorm(h, pn_ref[...]).astype(BF16)
    gate = _sigmoid(_dot(xn, wg_ref[...]) + bg_ref[...])
    h = h + gate * _dot(p_ref[...].astype(BF16), wp_ref[...])
    if final_norm:
        h = _rmsnorm(h, fn_ref[...])
    o_ref[...] = h


def _combine(dest3, h2, ri, p3, y_rows, pn, wg, bg, wp, fnorm, *, layer, tm, final_norm):
    T, D = h2.shape
    P = p3.shape[2]
    kern = functools.partial(_combine_kernel, final_norm=final_norm)
    row = lambda i: (0, 0)
    last = T // tm - 1
    return pl.pallas_call(
        kern,
        grid=(T // tm,),
        in_specs=[
            pl.BlockSpec((1, 1, MOE_TOP_K * tm), lambda i: (i, 0, 0), memory_space=pltpu.SMEM),
            pl.BlockSpec((1, 1, MOE_TOP_K * tm), lambda i: (jnp.minimum(i + 1, last), 0, 0),
                         memory_space=pltpu.SMEM),
            pl.BlockSpec((tm, D), lambda i: (i, 0)),
            pl.BlockSpec((tm, V7X_LANES), lambda i: (i, 0)),
            pl.BlockSpec((None, tm, P), lambda i: (layer, i, 0)),
            pl.BlockSpec(memory_space=pl.ANY),
            pl.BlockSpec((1, D), row),
            pl.BlockSpec((D, D), row),
            pl.BlockSpec((1, D), row),
            pl.BlockSpec((P, D), row),
            pl.BlockSpec((1, D), row),
        ],
        out_specs=pl.BlockSpec((tm, D), lambda i: (i, 0)),
        out_shape=jax.ShapeDtypeStruct((T, D), F32),
        scratch_shapes=[pltpu.VMEM((2, MOE_TOP_K, tm * V7X_SUBLANES, V7X_LANES), F32),
                        pltpu.SemaphoreType.DMA((2,))],
        compiler_params=_cparams(1),
        name="moe_combine_ple",
    )(dest3, dest3, h2, ri, p3, y_rows, pn, wg, bg, wp, fnorm)


TM_PROJ = 256
TM_RES = 512
TM_TOKEN = 256
EXPERT_ROWS = 512
ATTN_Q_BLOCK = 512
ATTN_K_BLOCK = 256


def _row(v):
    return v.reshape(1, -1).astype(F32)


def _moe_ple(h2, p3, moe_norm, w_rg, b_rg, w_re, b_re, w_gate, w_up, w_down,
             ple_norm, ple_w_gate, ple_b_gate, ple_w_proj, final_norm, *, layer, last):
    T, D = h2.shape
    G = w_rg.shape[1]
    E = w_re.shape[1]
    TK = T * MOE_TOP_K
    pad = V7X_LANES - G - E
    w_r = jnp.concatenate([w_rg, w_re, jnp.zeros((D, pad), F32)], axis=1)
    b_r = jnp.concatenate([b_rg, b_re, jnp.zeros((pad,), F32)]).reshape(1, V7X_LANES)
    w_hi = w_r.astype(BF16)
    w_mid = (w_r - w_hi.astype(F32)).astype(BF16)
    ri, cnt = _router(h2, _row(moe_norm), w_hi, w_mid, b_r, n_experts=E, tm=TM_TOKEN)

    rb = EXPERT_ROWS
    counts = cnt[0, G:G + E].astype(I32)
    padded = ((counts + rb - 1) // rb) * rb
    pad_end = jnp.cumsum(padded)
    pad_start = pad_end - padded
    n_blk = TK // rb + E
    n_rows = n_blk * rb
    eids = jnp.arange(E, dtype=I32)
    expert = ri[:, RI_E0:RI_E1 + 1].astype(I32)
    rank = ri[:, RI_R0:RI_R1 + 1].astype(I32)
    dest = jnp.sum(jnp.where(expert[:, :, None] == eids, pad_start, 0), axis=-1) + rank
    dest3 = dest.reshape(T // TM_TOKEN, 1, MOE_TOP_K * TM_TOKEN)
    blk_row0 = jnp.arange(n_blk, dtype=I32) * rb
    blk_expert = jnp.minimum(jnp.sum((pad_end[None, :] <= blk_row0[:, None]).astype(I32), axis=1), E - 1)
    n_used = (pad_end[-1:] // rb).astype(I32)

    x_rows = _dispatch(dest3, h2, _row(moe_norm), n_rows, tm=TM_TOKEN)
    y_rows = _experts(blk_expert, n_used, x_rows, w_gate, w_up, w_down, layer=layer, rb=rb)
    return _combine(dest3, h2, ri, p3, y_rows, _row(ple_norm), ple_w_gate.astype(BF16), _row(ple_b_gate),
                    ple_w_proj.astype(BF16), _row(final_norm), layer=layer, tm=TM_TOKEN, final_norm=last)


def _ssd_layer(h2, B, L, norm, w_in, conv_w, conv_b, dt_bias, a_log, d_skip, gnorm, w_out):
    T, D = h2.shape
    H = a_log.shape[0]
    d_inner = H * SSD_HEAD_DIM
    cdim = conv_w.shape[1]
    wz = w_in[:, :d_inner].astype(BF16)
    wx = w_in[:, d_inner:d_inner + cdim].astype(BF16)
    padh = V7X_LANES - H
    wdt = jnp.concatenate([w_in[:, d_inner + cdim:], jnp.zeros((D, padh), F32)], axis=1).astype(BF16)
    dtb = jnp.concatenate([dt_bias, jnp.zeros((padh,), F32)]).reshape(1, V7X_LANES)
    a_row = jnp.concatenate([-jnp.exp(a_log.astype(F32)), jnp.zeros((padh,), F32)]).reshape(1, V7X_LANES)
    dsk = jnp.repeat(d_skip.astype(F32), SSD_HEAD_DIM).reshape(1, d_inner)

    z, xbc, dt = _in_proj(h2, _row(norm), wz, wx, wdt, dtb, tm=TM_PROJ)
    yn = _ssd_scan(xbc.reshape(B, L, cdim), z.reshape(B, L, d_inner), dt.reshape(B, L, V7X_LANES),
                   conv_w.astype(F32), _row(conv_b), a_row, dsk, _row(gnorm), d_inner=d_inner)
    return _proj_res(yn.reshape(T, d_inner), w_out.astype(BF16), h2, tm=TM_RES)


def _sb_layer(h2, B, L, norm, w_qkv, w_o):
    T, D = h2.shape
    hd = SB_HEAD_DIM
    n_heads = w_o.shape[0] // hd
    q, k, v = _qkv_proj(h2.reshape(B, L, D), _row(norm), w_qkv.astype(BF16),
                        n_heads=n_heads, hd=hd, tm=TM_PROJ)
    bh = B * n_heads
    o = _attention(q.reshape(bh, L, hd), k.reshape(bh, L, hd), v.reshape(bh, L, hd),
                   B=B, n_heads=n_heads, tq=min(ATTN_Q_BLOCK, L), tk=ATTN_K_BLOCK)
    return _proj_res(o.reshape(T, n_heads * hd), w_o.astype(BF16), h2, tm=TM_RES)


def kernel(x, p, ssd_norm, ssd_w_in, ssd_conv_w, ssd_conv_b, ssd_dt_bias, ssd_a_log, ssd_d, ssd_gnorm, ssd_w_out, sb_norm, sb_w_qkv, sb_w_o, moe_norm, moe_w_rg, moe_b_rg, moe_w_re, moe_b_re, moe_w_gate, moe_w_up, moe_w_down, ple_norm, ple_w_gate, ple_b_gate, ple_w_proj, final_norm):
    B, L, D = x.shape
    depth = p.shape[0]
    T = B * L
    n_mixers = 2
    h = x.reshape(T, D)
    p3 = p.reshape(depth, T, p.shape[-1])
    for i in range(depth):
        j = i // n_mixers
        if i % n_mixers == 0:
            h = _ssd_layer(h, B, L, ssd_norm[j], ssd_w_in[j], ssd_conv_w[j], ssd_conv_b[j], ssd_dt_bias[j],
                           ssd_a_log[j], ssd_d[j], ssd_gnorm[j], ssd_w_out[j])
        else:
            h = _sb_layer(h, B, L, sb_norm[j], sb_w_qkv[j], sb_w_o[j])
        h = _moe_ple(h, p3, moe_norm[i], moe_w_rg[i], moe_b_rg[i], moe_w_re[i], moe_b_re[i],
                     moe_w_gate, moe_w_up, moe_w_down, ple_norm[i], ple_w_gate[i], ple_b_gate[i],
                     ple_w_proj[i], final_norm, layer=i, last=(i == depth - 1))
    return h.reshape(B, L, D)
```

```python
import functools
import math

import jax
import jax.numpy as jnp
from jax import lax
from jax.experimental import pallas as pl
from jax.experimental.pallas import tpu as pltpu

F32 = jnp.float32
BF16 = jnp.bfloat16
I32 = jnp.int32

EPS = 1e-6
V7X_LANES = 128
V7X_SUBLANES = 8
V7X_VMEM_LIMIT_BYTES = 56 * 1024 * 1024

SSD_HEAD_DIM = 64
SSD_N_GROUPS = 4
SSD_D_STATE = 128
SSD_CHUNK = 128
SSD_CONV = 4
SB_HEAD_DIM = 128
MOE_GROUPS = 8
MOE_TOP_K = 2

SCALAR_UNROLL = 8

ATTN_DEAD_LOG = -110.0


def _cparams(n_axes):
    return pltpu.CompilerParams(
        dimension_semantics=("arbitrary",) * n_axes,
        vmem_limit_bytes=V7X_VMEM_LIMIT_BYTES,
    )


def _dot(a, b):
    return jnp.dot(a, b, preferred_element_type=F32)


def _dot_nt(a, b):
    return lax.dot_general(a, b, (((1,), (1,)), ((), ())), preferred_element_type=F32)


def _split3(x):
    hi = x.astype(BF16)
    r1 = x - hi.astype(F32)
    mid = r1.astype(BF16)
    lo = (r1 - mid.astype(F32)).astype(BF16)
    return hi, mid, lo


def _split2(x):
    hi = x.astype(BF16)
    lo = (x - hi.astype(F32)).astype(BF16)
    return hi, lo


def _dot_exact_lhs(a_bf16, x):
    hi, mid, lo = _split3(x)
    return _dot(a_bf16, hi) + _dot(a_bf16, mid) + _dot(a_bf16, lo)


def _sigmoid(x):
    return 1.0 / (1.0 + jnp.exp(-x))


def _softplus(x):
    return jnp.maximum(x, 0.0) + jnp.log(1.0 + jnp.exp(-jnp.abs(x)))


def _rmsnorm(x, g):
    ms = jnp.mean(x * x, axis=-1, keepdims=True)
    return (x * lax.rsqrt(ms + EPS)) * g


def _tile_rows_load(ref, n):
    return jnp.concatenate(
        [ref[pl.ds(j, n, stride=V7X_SUBLANES), :] for j in range(V7X_SUBLANES)], axis=1)


def _tile_rows_store(ref, val, n):
    for j in range(V7X_SUBLANES):
        ref[pl.ds(j, n, stride=V7X_SUBLANES), :] = val[:, j * V7X_LANES:(j + 1) * V7X_LANES]


def _in_proj_kernel(x_ref, g_ref, wz_ref, wx_ref, wdt_ref, dtb_ref, z_ref, xbc_ref, dt_ref):
    xn = _rmsnorm(x_ref[...], g_ref[...]).astype(BF16)
    z_ref[...] = _dot(xn, wz_ref[...])
    xbc_ref[...] = _dot(xn, wx_ref[...])
    dt_ref[...] = _softplus(_dot(xn, wdt_ref[...]) + dtb_ref[...])


def _in_proj(x2, g, wz, wx, wdt, dtb, *, tm):
    T, D = x2.shape
    nz, nx, nd = wz.shape[1], wx.shape[1], wdt.shape[1]
    full = lambda i: (0, 0)
    return pl.pallas_call(
        _in_proj_kernel,
        grid=(T // tm,),
        in_specs=[
            pl.BlockSpec((tm, D), lambda i: (i, 0)),
            pl.BlockSpec((1, D), full),
            pl.BlockSpec((D, nz), full),
            pl.BlockSpec((D, nx), full),
            pl.BlockSpec((D, nd), full),
            pl.BlockSpec((1, nd), full),
        ],
        out_specs=[
            pl.BlockSpec((tm, nz), lambda i: (i, 0)),
            pl.BlockSpec((tm, nx), lambda i: (i, 0)),
            pl.BlockSpec((tm, nd), lambda i: (i, 0)),
        ],
        out_shape=[
            jax.ShapeDtypeStruct((T, nz), F32),
            jax.ShapeDtypeStruct((T, nx), F32),
            jax.ShapeDtypeStruct((T, nd), F32),
        ],
        compiler_params=_cparams(1),
        name="ssd_in_proj",
    )(x2, g, wz, wx, wdt, dtb)


def _ssd_kernel(xbc_ref, z_ref, dt_ref, cw_ref, cb_ref, a_ref, dsk_ref, gn_ref, e2_ref, yn_ref,
                state_ref, cbuf_ref, act_ref, yz_ref, exp_ref, *, Q, d_inner, n_groups, d_state, head_dim):
    c = pl.program_id(1)
    N = d_state
    gw = d_inner // n_groups
    pairs_per_group = gw // V7X_LANES
    halo = V7X_SUBLANES

    @pl.when(c == 0)
    def _init():
        state_ref[...] = jnp.zeros_like(state_ref)
        cbuf_ref[0:halo, :] = jnp.zeros((halo, cbuf_ref.shape[1]), F32)

    x_in = xbc_ref[...]
    cbuf_ref[halo:halo + Q, :] = x_in
    cw = cw_ref[...]
    conv = x_in * cw[SSD_CONV - 1:SSD_CONV, :] + cb_ref[...]
    for k in range(SSD_CONV - 1):
        off = halo - (SSD_CONV - 1) + k
        conv = conv + cbuf_ref[off:off + Q, :] * cw[k:k + 1, :]
    cbuf_ref[0:halo, :] = x_in[Q - halo:Q, :]
    act_ref[...] = conv * _sigmoid(conv)

    dt = dt_ref[...]
    rows = lax.broadcasted_iota(I32, (Q, Q), 0)
    cols = lax.broadcasted_iota(I32, (Q, Q), 1)
    causal = rows >= cols
    tril = jnp.where(causal, 1.0, 0.0).astype(BF16)
    acum = _dot_exact_lhs(tril, dt * a_ref[...])
    acum_t = acum.T
    a_last = acum[Q - 1:Q, :]
    e2 = e2_ref[...]
    for idx, v in enumerate((dt, jnp.exp(acum), jnp.exp(a_last - acum))):
        hi, lo = _split2(v)
        exp_ref[idx] = _dot(jnp.concatenate([hi, lo], axis=1), e2)
    DT, DIN, DTE = 0, 1, 2

    for g in range(n_groups):
        b_f = act_ref[:, d_inner + g * N:d_inner + (g + 1) * N]
        b_g = b_f.astype(BF16)
        c_g = act_ref[:, d_inner + n_groups * N + g * N:d_inner + n_groups * N + (g + 1) * N].astype(BF16)
        cb = _dot_nt(c_g, b_g)
        gsl = slice(g * gw, (g + 1) * gw)
        s_old = state_ref[:, gsl]
        y_off = _dot(c_g, s_old.astype(BF16)) * exp_ref[DIN, :, gsl]
        xdt_g = act_ref[:, gsl] * exp_ref[DT, :, gsl]
        for jj in range(pairs_per_group):
            j = g * pairs_per_group + jj
            h0 = 2 * j
            sl = slice(j * V7X_LANES, (j + 1) * V7X_LANES)
            lsl = slice(jj * V7X_LANES, (jj + 1) * V7X_LANES)
            xdt_b = xdt_g[:, lsl].astype(BF16)
            ys = []
            for hh in (h0, h0 + 1):
                seg = acum[:, hh:hh + 1] - acum_t[hh:hh + 1, :]
                dec = jnp.exp(jnp.where(causal, seg, -jnp.inf))
                ys.append(_dot((cb * dec).astype(BF16), xdt_b))
            lane = lax.broadcasted_iota(I32, (Q, V7X_LANES), 1)
            y = jnp.where(lane < head_dim, ys[0], ys[1])
            y = y + y_off[:, lsl] + act_ref[:, sl] * dsk_ref[:, sl]
            zp = z_ref[:, sl]
            yz_ref[:, sl] = y * (zp * _sigmoid(zp))
        xw_g = (xdt_g * exp_ref[DTE, :, gsl]).astype(BF16)
        new_states = _dot(b_f.T.astype(BF16), xw_g)
        state_ref[:, gsl] = s_old * exp_ref[DIN, Q - 1:Q, gsl] + new_states
        yzg = yz_ref[:, g * gw:(g + 1) * gw]
        ms = jnp.mean(yzg * yzg, axis=-1, keepdims=True)
        yn_ref[:, g * gw:(g + 1) * gw] = (
            (yzg * lax.rsqrt(ms + EPS)) * gn_ref[:, g * gw:(g + 1) * gw]).astype(yn_ref.dtype)


def _ssd_scan(xbc, z, dt, cw, cb, a_row, dsk, gn, *, d_inner):
    B, L, cdim = xbc.shape
    Q = SSD_CHUNK
    nd = dt.shape[-1]
    full2 = lambda b, c: (0, 0)
    head_of_col = jnp.arange(d_inner, dtype=I32) // SSD_HEAD_DIM
    e1 = (jnp.arange(nd, dtype=I32)[:, None] == head_of_col[None, :]).astype(BF16)
    e2 = jnp.concatenate([e1, e1], axis=0)
    kern = functools.partial(_ssd_kernel, Q=Q, d_inner=d_inner, n_groups=SSD_N_GROUPS,
                             d_state=SSD_D_STATE, head_dim=SSD_HEAD_DIM)
    return pl.pallas_call(
        kern,
        grid=(B, L // Q),
        in_specs=[
            pl.BlockSpec((None, Q, cdim), lambda b, c: (b, c, 0)),
            pl.BlockSpec((None, Q, d_inner), lambda b, c: (b, c, 0)),
            pl.BlockSpec((None, Q, nd), lambda b, c: (b, c, 0)),
            pl.BlockSpec(cw.shape, full2),
            pl.BlockSpec(cb.shape, full2),
            pl.BlockSpec(a_row.shape, full2),
            pl.BlockSpec(dsk.shape, full2),
            pl.BlockSpec(gn.shape, full2),
            pl.BlockSpec(e2.shape, full2),
        ],
        out_specs=pl.BlockSpec((None, Q, d_inner), lambda b, c: (b, c, 0)),
        out_shape=jax.ShapeDtypeStruct((B, L, d_inner), BF16),
        scratch_shapes=[
            pltpu.VMEM((SSD_D_STATE, d_inner), F32),
            pltpu.VMEM((Q + V7X_SUBLANES, cdim), F32),
            pltpu.VMEM((Q, cdim), F32),
            pltpu.VMEM((Q, d_inner), F32),
            pltpu.VMEM((3, Q, d_inner), F32),
        ],
        compiler_params=_cparams(2),
        name="ssd_scan",
    )(xbc, z, dt, cw, cb, a_row, dsk, gn, e2)


def _proj_res_kernel(a_ref, w_ref, r_ref, o_ref):
    o_ref[...] = r_ref[...] + _dot(a_ref[...], w_ref[...])


def _proj_res(a, w, res, *, tm):
    T, K = a.shape
    N = w.shape[1]
    return pl.pallas_call(
        _proj_res_kernel,
        grid=(T // tm,),
        in_specs=[
            pl.BlockSpec((tm, K), lambda i: (i, 0)),
            pl.BlockSpec((K, N), lambda i: (0, 0)),
            pl.BlockSpec((tm, N), lambda i: (i, 0)),
        ],
        out_specs=pl.BlockSpec((tm, N), lambda i: (i, 0)),
        out_shape=jax.ShapeDtypeStruct((T, N), F32),
        compiler_params=_cparams(1),
        name="proj_residual",
    )(a, w, res)


def _qkv_kernel(x_ref, g_ref, w_ref, q_ref, k_ref, v_ref, *, n_heads, hd, scale):
    xn = _rmsnorm(x_ref[...], g_ref[...]).astype(BF16)
    qkv = _dot(xn, w_ref[...])
    d = n_heads * hd
    for h in range(n_heads):
        q_ref[h] = (qkv[:, h * hd:(h + 1) * hd] * scale).astype(BF16)
        k_ref[h] = qkv[:, d + h * hd:d + (h + 1) * hd].astype(BF16)
        v_ref[h] = qkv[:, 2 * d + h * hd:2 * d + (h + 1) * hd].astype(BF16)


def _qkv_proj(x, g, w, *, n_heads, hd, tm):
    B, L, D = x.shape
    kern = functools.partial(_qkv_kernel, n_heads=n_heads, hd=hd, scale=1.0 / math.sqrt(hd))
    hspec = pl.BlockSpec((None, n_heads, tm, hd), lambda b, i: (b, 0, i, 0))
    hshape = jax.ShapeDtypeStruct((B, n_heads, L, hd), BF16)
    return pl.pallas_call(
        kern,
        grid=(B, L // tm),
        in_specs=[
            pl.BlockSpec((None, tm, D), lambda b, i: (b, i, 0)),
            pl.BlockSpec((1, D), lambda b, i: (0, 0)),
            pl.BlockSpec(w.shape, lambda b, i: (0, 0)),
        ],
        out_specs=[hspec, hspec, hspec],
        out_shape=[hshape, hshape, hshape],
        compiler_params=_cparams(2),
        name="sb_qkv_proj",
    )(x, g, w)


def _attn_kernel(q_ref, k_ref, v_ref, o_ref, acc_ref, carry_ref, *, tq, tk):
    qi = pl.program_id(1)
    nh, _, hd = q_ref.shape
    n_diag = tq // tk
    rows_t = lax.broadcasted_iota(I32, (tk, tk), 0)
    cols_t = lax.broadcasted_iota(I32, (tk, tk), 1)
    tri = jnp.where(rows_t >= cols_t, 1.0, 0.0).astype(BF16)
    tri2 = jnp.concatenate([tri, tri], axis=0)

    def block(k0, r0, diag):
        nr = tq - r0
        if diag:
            before = (lax.broadcasted_iota(I32, (nr, tk), 1) < lax.broadcasted_iota(I32, (nr, tk), 0))
        for h in range(nh):
            q = q_ref[h, r0:tq, :]
            k = k_ref[h, pl.ds(k0, tk), :]
            v = v_ref[h, pl.ds(k0, tk), :]
            z = _dot_nt(q, k)
            sp = _softplus(z)
            if diag:
                sp = jnp.where(before, sp, 0.0)
            hi, lo = _split2(sp)
            cs = _dot(jnp.concatenate([hi, lo], axis=1), tri2)
            carry = carry_ref[h, r0:tq, 0:1]
            log_a = z - cs + carry
            if diag:
                log_a = jnp.where(before, log_a, -jnp.inf)
            a = jnp.exp(log_a)
            acc_ref[h, r0:tq, :] += _dot(a.astype(BF16), v)
            carry_ref[h, r0:tq, :] = jnp.broadcast_to(carry - cs[:, 0:1], (nr, carry_ref.shape[2]))

    acc_ref[...] = jnp.zeros_like(acc_ref)
    carry_ref[...] = jnp.zeros_like(carry_ref)
    q0 = qi * tq
    for d in range(n_diag - 1, -1, -1):
        block(pl.multiple_of(q0 + d * tk, tk), d * tk, True)

    def cond(st):
        kb, live = st
        return jnp.logical_and(kb >= 0, live > ATTN_DEAD_LOG)

    def body(st):
        kb, _ = st
        block(pl.multiple_of(kb * tk, tk), 0, False)
        return kb - 1, jnp.max(carry_ref[:, :, 0:1])

    lax.while_loop(cond, body, (qi * n_diag - 1, jnp.max(carry_ref[:, :, 0:1])))
    for h in range(nh):
        o_ref[:, h * hd:(h + 1) * hd] = acc_ref[h].astype(o_ref.dtype)


def _attention(q, k, v, *, B, n_heads, tq, tk, nh):
    BH, L, hd = q.shape
    kern = functools.partial(_attn_kernel, tq=tq, tk=tk)
    groups = n_heads // nh
    return pl.pallas_call(
        kern,
        grid=(BH // nh, L // tq),
        in_specs=[
            pl.BlockSpec((nh, tq, hd), lambda g, i: (g, i, 0)),
            pl.BlockSpec((nh, L, hd), lambda g, i: (g, 0, 0)),
            pl.BlockSpec((nh, L, hd), lambda g, i: (g, 0, 0)),
        ],
        out_specs=pl.BlockSpec((None, tq, nh * hd), lambda g, i: (g // groups, i, g % groups)),
        out_shape=jax.ShapeDtypeStruct((B, L, n_heads * hd), BF16),
        scratch_shapes=[pltpu.VMEM((nh, tq, hd), F32), pltpu.VMEM((nh, tq, V7X_LANES), F32)],
        compiler_params=_cparams(2),
        name="sb_attention",
    )(q, k, v)


RI_E0, RI_E1, RI_G0, RI_G1, RI_R0, RI_R1 = 0, 1, 2, 3, 4, 5


def _router_kernel(h_ref, g_ref, whm_ref, b_ref, ri_ref, cnt_ref, carry_ref, *, n_groups, n_experts):
    i = pl.program_id(0)
    tm = h_ref.shape[0]
    epg = n_experts // n_groups

    @pl.when(i == 0)
    def _init():
        carry_ref[...] = jnp.zeros_like(carry_ref)

    xn = _rmsnorm(h_ref[...], g_ref[...])
    xh, xm = _split2(xn)
    both = _dot(xh, whm_ref[...])
    logits = (both[:, :V7X_LANES] + (both[:, V7X_LANES:] + _dot(xm, whm_ref[:, :V7X_LANES]))) + b_ref[...]

    lane = lax.broadcasted_iota(I32, (tm, V7X_LANES), 1).astype(F32)
    neg = -jnp.inf
    no_lane = float(V7X_LANES)
    gl = jnp.where(lane < n_groups, logits, neg)
    gmax = jnp.max(gl, axis=-1, keepdims=True)
    gidx = jnp.min(jnp.where(gl == gmax, lane, no_lane), axis=-1, keepdims=True)
    g_w = 1.0 / jnp.sum(jnp.exp(gl - gmax), axis=-1, keepdims=True)

    first = n_groups + gidx * epg
    in_group = jnp.logical_and(lane >= first, lane < first + epg)
    el = jnp.where(in_group, logits, neg)
    m0 = jnp.max(el, axis=-1, keepdims=True)
    i0 = jnp.min(jnp.where(el == m0, lane, no_lane), axis=-1, keepdims=True)
    el1 = jnp.where(lane == i0, neg, el)
    m1 = jnp.max(el1, axis=-1, keepdims=True)
    i1 = jnp.min(jnp.where(el1 == m1, lane, no_lane), axis=-1, keepdims=True)
    d = jnp.exp(m1 - m0)
    p0 = 1.0 / (1.0 + d)
    gate0 = g_w * p0
    gate1 = g_w * (d * p0)

    oh0 = jnp.where(lane == i0, 1.0, 0.0)
    oh1 = jnp.where(lane == i1, 1.0, 0.0)
    r = lax.broadcasted_iota(I32, (tm, tm), 0)
    cc = lax.broadcasted_iota(I32, (tm, tm), 1)
    strict = jnp.where(cc < r, 1.0, 0.0).astype(BF16)
    pre0 = _dot(strict, oh0.astype(BF16))
    pre1 = _dot(strict, oh1.astype(BF16))
    cnt0 = jnp.sum(oh0, axis=0, keepdims=True)
    cnt1 = jnp.sum(oh1, axis=0, keepdims=True)
    base = carry_ref[0:1, :]
    rank0 = jnp.sum(oh0 * (pre0 + base), axis=-1, keepdims=True)
    rank1 = jnp.sum(oh1 * (pre1 + (base + cnt0)), axis=-1, keepdims=True)
    total = base + cnt0 + cnt1
    carry_ref[...] = jnp.broadcast_to(total, carry_ref.shape)
    cnt_ref[...] = jnp.broadcast_to(total, cnt_ref.shape)

    e0 = i0 - n_groups
    e1 = i1 - n_groups
    ri = jnp.where(lane == RI_E0, e0, 0.0)
    ri = jnp.where(lane == RI_E1, e1, ri)
    ri = jnp.where(lane == RI_G0, gate0, ri)
    ri = jnp.where(lane == RI_G1, gate1, ri)
    ri = jnp.where(lane == RI_R0, rank0, ri)
    ri = jnp.where(lane == RI_R1, rank1, ri)
    ri_ref[...] = ri


def _router(h2, g, w_hm, b_r, *, n_experts, tm):
    T, D = h2.shape
    kern = functools.partial(_router_kernel, n_groups=MOE_GROUPS, n_experts=n_experts)
    return pl.pallas_call(
        kern,
        grid=(T // tm,),
        in_specs=[
            pl.BlockSpec((tm, D), lambda i: (i, 0)),
            pl.BlockSpec((1, D), lambda i: (0, 0)),
            pl.BlockSpec((D, 2 * V7X_LANES), lambda i: (0, 0)),
            pl.BlockSpec((1, V7X_LANES), lambda i: (0, 0)),
        ],
        out_specs=[
            pl.BlockSpec((tm, V7X_LANES), lambda i: (i, 0)),
            pl.BlockSpec((V7X_SUBLANES, V7X_LANES), lambda i: (0, 0)),
        ],
        out_shape=[
            jax.ShapeDtypeStruct((T, V7X_LANES), F32),
            jax.ShapeDtypeStruct((V7X_SUBLANES, V7X_LANES), F32),
        ],
        scratch_shapes=[pltpu.VMEM((V7X_SUBLANES, V7X_LANES), F32)],
        compiler_params=_cparams(1),
        name="moe_router",
    )(h2, g, w_hm, b_r)


def _tile_copy(src_ref, src_row, dst_ref, dst_row, sem):
    s = pl.multiple_of(src_row * V7X_SUBLANES, V7X_SUBLANES)
    d = pl.multiple_of(dst_row * V7X_SUBLANES, V7X_SUBLANES)
    return pltpu.make_async_copy(src_ref.at[pl.ds(s, V7X_SUBLANES)], dst_ref.at[pl.ds(d, V7X_SUBLANES)], sem)


def _token_loop(tm, fn):
    def step(t, c):
        fn(t)
        return c

    lax.fori_loop(0, tm, step, 0, unroll=SCALAR_UNROLL)


def _dispatch_kernel(dest_ref, h_ref, g_ref, init_ref, rows_ref, xt_ref, sem):
    del init_ref
    i = pl.program_id(0)
    tm = h_ref.shape[0]
    cur = i % 2
    _tile_rows_store(xt_ref.at[cur], _rmsnorm(h_ref[...], g_ref[...]), tm)

    def issue(t):
        for k in range(MOE_TOP_K):
            _tile_copy(xt_ref.at[cur], t, rows_ref, dest_ref[0, 0, MOE_TOP_K * t + k],
                       sem.at[cur]).start(priority=k)

    def drain(b):
        def wait(t):
            for _ in range(MOE_TOP_K):
                _tile_copy(xt_ref.at[b], t, rows_ref, 0, sem.at[b]).wait()

        _token_loop(tm, wait)

    _token_loop(tm, issue)

    @pl.when(i >= 1)
    def _prev():
        drain(1 - cur)

    @pl.when(i == pl.num_programs(0) - 1)
    def _last():
        drain(cur)


def _dispatch(dest3, h2, g, n_rows, *, tm):
    T, D = h2.shape
    tiles = (n_rows * V7X_SUBLANES, V7X_LANES)
    init = jnp.zeros(tiles, F32)
    return pl.pallas_call(
        _dispatch_kernel,
        grid=(T // tm,),
        in_specs=[
            pl.BlockSpec((1, 1, MOE_TOP_K * tm), lambda i: (i, 0, 0), memory_space=pltpu.SMEM),
            pl.BlockSpec((tm, D), lambda i: (i, 0)),
            pl.BlockSpec((1, D), lambda i: (0, 0)),
            pl.BlockSpec(memory_space=pl.ANY),
        ],
        out_specs=pl.BlockSpec(memory_space=pl.ANY),
        out_shape=jax.ShapeDtypeStruct(tiles, F32),
        scratch_shapes=[pltpu.VMEM((2, tm * V7X_SUBLANES, V7X_LANES), F32), pltpu.SemaphoreType.DMA((2,))],
        input_output_aliases={3: 0},
        compiler_params=_cparams(1),
        name="moe_dispatch",
    )(dest3, h2, g, init)


def _expert_kernel(be_ref, nu_ref, x_ref, wg_ref, wu_ref, wd_ref, y_ref):
    i = pl.program_id(0)
    rb = x_ref.shape[0] // V7X_SUBLANES

    @pl.when(i < nu_ref[0])
    def _live():
        x = _tile_rows_load(x_ref, rb).astype(BF16)
        gt = _dot(x, wg_ref[0].astype(BF16))
        up = _dot(x, wu_ref[0].astype(BF16))
        act = (gt * _sigmoid(gt)) * up
        _tile_rows_store(y_ref, _dot(act.astype(BF16), wd_ref[0].astype(BF16)), rb)

    @pl.when(i >= nu_ref[0])
    def _dead():
        y_ref[...] = jnp.zeros_like(y_ref)


def _experts(blk_expert, n_used, x_rows, w_gate, w_up, w_down, *, layer, rb):
    n_blk = x_rows.shape[0] // (rb * V7X_SUBLANES)
    _, E, D, F = w_gate.shape
    tile_blk = (rb * V7X_SUBLANES, V7X_LANES)

    def row_map(i, be, nu):
        return (jnp.minimum(i, nu[0] - 1), 0)

    def w_map(i, be, nu):
        return (layer, be[i], 0, 0)

    grid_spec = pltpu.PrefetchScalarGridSpec(
        num_scalar_prefetch=2,
        grid=(n_blk,),
        in_specs=[
            pl.BlockSpec(tile_blk, row_map),
            pl.BlockSpec((None, 1, D, F), w_map),
            pl.BlockSpec((None, 1, D, F), w_map),
            pl.BlockSpec((None, 1, F, D), w_map),
        ],
        out_specs=pl.BlockSpec(tile_blk, lambda i, be, nu: (i, 0)),
    )
    return pl.pallas_call(
        _expert_kernel,
        grid_spec=grid_spec,
        out_shape=jax.ShapeDtypeStruct(x_rows.shape, F32),
        compiler_params=_cparams(1),
        name="moe_experts",
    )(blk_expert, n_used, x_rows, w_gate, w_up, w_down)


def _combine_kernel(dest_ref, destn_ref, h_ref, ri_ref, p_ref, yrows_ref, pn_ref, wg_ref, bg_ref, wp_ref,
                    fn_ref, o_ref, ybuf_ref, sem, *, final_norm):
    i = pl.program_id(0)
    tm = h_ref.shape[0]
    cur = i % 2

    def fetch(d_ref, b):
        def issue(t):
            for k in range(MOE_TOP_K):
                _tile_copy(yrows_ref, d_ref[0, 0, MOE_TOP_K * t + k], ybuf_ref.at[b, k], t,
                           sem.at[b]).start(priority=k)

        _token_loop(tm, issue)

    @pl.when(i == 0)
    def _first():
        fetch(dest_ref, 0)

    @pl.when(i + 1 < pl.num_programs(0))
    def _next():
        fetch(destn_ref, 1 - cur)

    def wait(t):
        for k in range(MOE_TOP_K):
            _tile_copy(yrows_ref, 0, ybuf_ref.at[cur, k], t, sem.at[cur]).wait()

    _token_loop(tm, wait)

    ri = ri_ref[...]
    h = h_ref[...] + (_tile_rows_load(ybuf_ref.at[cur, 0], tm) * ri[:, RI_G0:RI_G0 + 1]
                      + _tile_rows_load(ybuf_ref.at[cur, 1], tm) * ri[:, RI_G1:RI_G1 + 1])
    xn = _rmsnorm(h, pn_ref[...]).astype(BF16)
    gate = _sigmoid(_dot(xn, wg_ref[...]) + bg_ref[...])
    h = h + gate * _dot(p_ref[...].astype(BF16), wp_ref[...])
    if final_norm:
        h = _rmsnorm(h, fn_ref[...])
    o_ref[...] = h


def _combine(dest3, h2, ri, p3, y_rows, pn, wg, bg, wp, fnorm, *, layer, tm, final_norm):
    T, D = h2.shape
    P = p3.shape[2]
    kern = functools.partial(_combine_kernel, final_norm=final_norm)
    row = lambda i: (0, 0)
    last = T // tm - 1
    return pl.pallas_call(
        kern,
        grid=(T // tm,),
        in_specs=[
            pl.BlockSpec((1, 1, MOE_TOP_K * tm), lambda i: (i, 0, 0), memory_space=pltpu.SMEM),
            pl.BlockSpec((1, 1, MOE_TOP_K * tm), lambda i: (jnp.minimum(i + 1, last), 0, 0),
                         memory_space=pltpu.SMEM),
            pl.BlockSpec((tm, D), lambda i: (i, 0)),
            pl.BlockSpec((tm, V7X_LANES), lambda i: (i, 0)),
            pl.BlockSpec((None, tm, P), lambda i: (layer, i, 0)),
            pl.BlockSpec(memory_space=pl.ANY),
            pl.BlockSpec((1, D), row),
            pl.BlockSpec((D, D), row),
            pl.BlockSpec((1, D), row),
            pl.BlockSpec((P, D), row),
            pl.BlockSpec((1, D), row),
        ],
        out_specs=pl.BlockSpec((tm, D), lambda i: (i, 0)),
        out_shape=jax.ShapeDtypeStruct((T, D), F32),
        scratch_shapes=[pltpu.VMEM((2, MOE_TOP_K, tm * V7X_SUBLANES, V7X_LANES), F32),
                        pltpu.SemaphoreType.DMA((2,))],
        compiler_params=_cparams(1),
        name="moe_combine_ple",
    )(dest3, dest3, h2, ri, p3, y_rows, pn, wg, bg, wp, fnorm)


TM_PROJ = 256
TM_RES = 512
TM_TOKEN = 256
EXPERT_ROWS = 512
ATTN_Q_BLOCK = 512
ATTN_K_BLOCK = 256
ATTN_HEADS = 2

def _row(v):
    return v.reshape(1, -1).astype(F32)


def _moe_ple(h2, p3, moe_norm, w_rg, b_rg, w_re, b_re, w_gate, w_up, w_down,
             ple_norm, ple_w_gate, ple_b_gate, ple_w_proj, final_norm, *, layer, last):
    T, D = h2.shape
    G = w_rg.shape[1]
    E = w_re.shape[1]
    TK = T * MOE_TOP_K
    pad = V7X_LANES - G - E
    w_r = jnp.concatenate([w_rg, w_re, jnp.zeros((D, pad), F32)], axis=1)
    b_r = jnp.concatenate([b_rg, b_re, jnp.zeros((pad,), F32)]).reshape(1, V7X_LANES)
    w_hi = w_r.astype(BF16)
    w_mid = (w_r - w_hi.astype(F32)).astype(BF16)
    ri, cnt = _router(h2, _row(moe_norm), jnp.concatenate([w_hi, w_mid], axis=1), b_r,
                      n_experts=E, tm=TM_TOKEN)

    rb = EXPERT_ROWS
    counts = cnt[0, G:G + E].astype(I32)
    padded = ((counts + rb - 1) // rb) * rb
    pad_end = jnp.cumsum(padded)
    pad_start = pad_end - padded
    n_blk = TK // rb + E
    n_rows = n_blk * rb
    eids = jnp.arange(E, dtype=I32)
    expert = ri[:, RI_E0:RI_E1 + 1].astype(I32)
    rank = ri[:, RI_R0:RI_R1 + 1].astype(I32)
    dest = jnp.sum(jnp.where(expert[:, :, None] == eids, pad_start, 0), axis=-1) + rank
    dest3 = dest.reshape(T // TM_TOKEN, 1, MOE_TOP_K * TM_TOKEN)
    blk_row0 = jnp.arange(n_blk, dtype=I32) * rb
    blk_expert = jnp.minimum(jnp.sum((pad_end[None, :] <= blk_row0[:, None]).astype(I32), axis=1), E - 1)
    n_used = (pad_end[-1:] // rb).astype(I32)

    x_rows = _dispatch(dest3, h2, _row(moe_norm), n_rows, tm=TM_TOKEN)
    y_rows = _experts(blk_expert, n_used, x_rows, w_gate, w_up, w_down, layer=layer, rb=rb)
    return _combine(dest3, h2, ri, p3, y_rows, _row(ple_norm), ple_w_gate.astype(BF16), _row(ple_b_gate),
                    ple_w_proj.astype(BF16), _row(final_norm), layer=layer, tm=TM_TOKEN, final_norm=last)


def _ssd_layer(h2, B, L, norm, w_in, conv_w, conv_b, dt_bias, a_log, d_skip, gnorm, w_out):
    T, D = h2.shape
    H = a_log.shape[0]
    d_inner = H * SSD_HEAD_DIM
    cdim = conv_w.shape[1]
    wz = w_in[:, :d_inner].astype(BF16)
    wx = w_in[:, d_inner:d_inner + cdim].astype(BF16)
    padh = V7X_LANES - H
    wdt = jnp.concatenate([w_in[:, d_inner + cdim:], jnp.zeros((D, padh), F32)], axis=1).astype(BF16)
    dtb = jnp.concatenate([dt_bias, jnp.zeros((padh,), F32)]).reshape(1, V7X_LANES)
    a_row = jnp.concatenate([-jnp.exp(a_log.astype(F32)), jnp.zeros((padh,), F32)]).reshape(1, V7X_LANES)
    dsk = jnp.repeat(d_skip.astype(F32), SSD_HEAD_DIM).reshape(1, d_inner)

    z, xbc, dt = _in_proj(h2, _row(norm), wz, wx, wdt, dtb, tm=TM_PROJ)
    yn = _ssd_scan(xbc.reshape(B, L, cdim), z.reshape(B, L, d_inner), dt.reshape(B, L, V7X_LANES),
                   conv_w.astype(F32), _row(conv_b), a_row, dsk, _row(gnorm), d_inner=d_inner)
    return _proj_res(yn.reshape(T, d_inner), w_out.astype(BF16), h2, tm=TM_RES)


def _sb_layer(h2, B, L, norm, w_qkv, w_o):
    T, D = h2.shape
    hd = SB_HEAD_DIM
    n_heads = w_o.shape[0] // hd
    q, k, v = _qkv_proj(h2.reshape(B, L, D), _row(norm), w_qkv.astype(BF16),
                        n_heads=n_heads, hd=hd, tm=TM_PROJ)
    bh = B * n_heads
    o = _attention(q.reshape(bh, L, hd), k.reshape(bh, L, hd), v.reshape(bh, L, hd),
                   B=B, n_heads=n_heads, tq=min(ATTN_Q_BLOCK, L), tk=ATTN_K_BLOCK, nh=ATTN_HEADS)
    return _proj_res(o.reshape(T, n_heads * hd), w_o.astype(BF16), h2, tm=TM_RES)


def kernel(x, p, ssd_norm, ssd_w_in, ssd_conv_w, ssd_conv_b, ssd_dt_bias, ssd_a_log, ssd_d, ssd_gnorm, ssd_w_out, sb_norm, sb_w_qkv, sb_w_o, moe_norm, moe_w_rg, moe_b_rg, moe_w_re, moe_b_re, moe_w_gate, moe_w_up, moe_w_down, ple_norm, ple_w_gate, ple_b_gate, ple_w_proj, final_norm):
    B, L, D = x.shape
    depth = p.shape[0]
    T = B * L
    n_mixers = 2
    h = x.reshape(T, D)
    p3 = p.reshape(depth, T, p.shape[-1])
    for i in range(depth):
        j = i // n_mixers
        if i % n_mixers == 0:
            h = _ssd_layer(h, B, L, ssd_norm[j], ssd_w_in[j], ssd_conv_w[j], ssd_conv_b[j], ssd_dt_bias[j],
                           ssd_a_log[j], ssd_d[j], ssd_gnorm[j], ssd_w_out[j])
        else:
            h = _sb_layer(h, B, L, sb_norm[j], sb_w_qkv[j], sb_w_o[j])
        h = _moe_ple(h, p3, moe_norm[i], moe_w_rg[i], moe_b_rg[i], moe_w_re[i], moe_b_re[i],
                     moe_w_gate, moe_w_up, moe_w_down, ple_norm[i], ple_w_gate[i], ple_b_gate[i],
                     ple_w_proj[i], final_norm, layer=i, last=(i == depth - 1))
    return h.reshape(B, L, D)
```

```python
import functools
import math

import jax
import jax.numpy as jnp
from jax import lax
from jax.experimental import pallas as pl
from jax.experimental.pallas import tpu as pltpu

F32 = jnp.float32
BF16 = jnp.bfloat16
I32 = jnp.int32

EPS = 1e-6
V7X_LANES = 128
V7X_SUBLANES = 8
V7X_VMEM_LIMIT_BYTES = 56 * 1024 * 1024

SSD_HEAD_DIM = 64
SSD_N_GROUPS = 4
SSD_D_STATE = 128
SSD_CHUNK = 128
SSD_CONV = 4
SB_HEAD_DIM = 128
MOE_GROUPS = 8
MOE_TOP_K = 2

SCALAR_UNROLL = 8

ATTN_DEAD_LOG = -110.0


def _cparams(n_axes):
    return pltpu.CompilerParams(
        dimension_semantics=("arbitrary",) * n_axes,
        vmem_limit_bytes=V7X_VMEM_LIMIT_BYTES,
    )


def _dot(a, b):
    return jnp.dot(a, b, preferred_element_type=F32)


def _dot_nt(a, b):
    return lax.dot_general(a, b, (((1,), (1,)), ((), ())), preferred_element_type=F32)


def _split3(x):
    hi = x.astype(BF16)
    r1 = x - hi.astype(F32)
    mid = r1.astype(BF16)
    lo = (r1 - mid.astype(F32)).astype(BF16)
    return hi, mid, lo


def _split2(x):
    hi = x.astype(BF16)
    lo = (x - hi.astype(F32)).astype(BF16)
    return hi, lo


def _dot_exact_lhs(a_bf16, x):
    hi, mid, lo = _split3(x)
    return _dot(a_bf16, hi) + _dot(a_bf16, mid) + _dot(a_bf16, lo)


def _sigmoid(x):
    return 1.0 / (1.0 + jnp.exp(-x))


def _softplus(x):
    return jnp.maximum(x, 0.0) + jnp.log(1.0 + jnp.exp(-jnp.abs(x)))


def _rmsnorm(x, g):
    ms = jnp.mean(x * x, axis=-1, keepdims=True)
    return (x * lax.rsqrt(ms + EPS)) * g


def _tile_rows_load(ref, n):
    return jnp.concatenate(
        [ref[pl.ds(j, n, stride=V7X_SUBLANES), :] for j in range(V7X_SUBLANES)], axis=1)


def _tile_rows_store(ref, val, n):
    for j in range(V7X_SUBLANES):
        ref[pl.ds(j, n, stride=V7X_SUBLANES), :] = val[:, j * V7X_LANES:(j + 1) * V7X_LANES]


def _in_proj_kernel(x_ref, g_ref, wz_ref, wx_ref, wdt_ref, dtb_ref, z_ref, xbc_ref, dt_ref):
    xn = _rmsnorm(x_ref[...], g_ref[...]).astype(BF16)
    z_ref[...] = _dot(xn, wz_ref[...])
    xbc_ref[...] = _dot(xn, wx_ref[...])
    dt_ref[...] = _softplus(_dot(xn, wdt_ref[...]) + dtb_ref[...])


def _in_proj(x2, g, wz, wx, wdt, dtb, *, tm):
    T, D = x2.shape
    nz, nx, nd = wz.shape[1], wx.shape[1], wdt.shape[1]
    full = lambda i: (0, 0)
    return pl.pallas_call(
        _in_proj_kernel,
        grid=(T // tm,),
        in_specs=[
            pl.BlockSpec((tm, D), lambda i: (i, 0)),
            pl.BlockSpec((1, D), full),
            pl.BlockSpec((D, nz), full),
            pl.BlockSpec((D, nx), full),
            pl.BlockSpec((D, nd), full),
            pl.BlockSpec((1, nd), full),
        ],
        out_specs=[
            pl.BlockSpec((tm, nz), lambda i: (i, 0)),
            pl.BlockSpec((tm, nx), lambda i: (i, 0)),
            pl.BlockSpec((tm, nd), lambda i: (i, 0)),
        ],
        out_shape=[
            jax.ShapeDtypeStruct((T, nz), F32),
            jax.ShapeDtypeStruct((T, nx), F32),
            jax.ShapeDtypeStruct((T, nd), F32),
        ],
        compiler_params=_cparams(1),
        name="ssd_in_proj",
    )(x2, g, wz, wx, wdt, dtb)


def _ssd_kernel(xbc_ref, z_ref, dt_ref, cw_ref, cb_ref, a_ref, dsk_ref, gn_ref, e2_ref, yn_ref,
                state_ref, cbuf_ref, act_ref, yz_ref, exp_ref, *, Q, d_inner, n_groups, d_state, head_dim):
    c = pl.program_id(1)
    N = d_state
    gw = d_inner // n_groups
    pairs_per_group = gw // V7X_LANES
    halo = V7X_SUBLANES

    @pl.when(c == 0)
    def _init():
        state_ref[...] = jnp.zeros_like(state_ref)
        cbuf_ref[0:halo, :] = jnp.zeros((halo, cbuf_ref.shape[1]), F32)

    x_in = xbc_ref[...]
    cbuf_ref[halo:halo + Q, :] = x_in
    cw = cw_ref[...]
    conv = x_in * cw[SSD_CONV - 1:SSD_CONV, :] + cb_ref[...]
    for k in range(SSD_CONV - 1):
        off = halo - (SSD_CONV - 1) + k
        conv = conv + cbuf_ref[off:off + Q, :] * cw[k:k + 1, :]
    cbuf_ref[0:halo, :] = x_in[Q - halo:Q, :]
    act_ref[...] = conv * _sigmoid(conv)

    dt = dt_ref[...]
    rows = lax.broadcasted_iota(I32, (Q, Q), 0)
    cols = lax.broadcasted_iota(I32, (Q, Q), 1)
    causal = rows >= cols
    tril = jnp.where(causal, 1.0, 0.0).astype(BF16)
    acum = _dot_exact_lhs(tril, dt * a_ref[...])
    acum_t = acum.T
    a_last = acum[Q - 1:Q, :]
    e2 = e2_ref[...]
    for idx, v in enumerate((dt, jnp.exp(acum), jnp.exp(a_last - acum))):
        hi, lo = _split2(v)
        exp_ref[idx] = _dot(jnp.concatenate([hi, lo], axis=1), e2)
    DT, DIN, DTE = 0, 1, 2

    for g in range(n_groups):
        b_f = act_ref[:, d_inner + g * N:d_inner + (g + 1) * N]
        b_g = b_f.astype(BF16)
        c_g = act_ref[:, d_inner + n_groups * N + g * N:d_inner + n_groups * N + (g + 1) * N].astype(BF16)
        cb = _dot_nt(c_g, b_g)
        gsl = slice(g * gw, (g + 1) * gw)
        s_old = state_ref[:, gsl]
        y_off = _dot(c_g, s_old.astype(BF16)) * exp_ref[DIN, :, gsl]
        xdt_g = act_ref[:, gsl] * exp_ref[DT, :, gsl]
        for jj in range(pairs_per_group):
            j = g * pairs_per_group + jj
            h0 = 2 * j
            sl = slice(j * V7X_LANES, (j + 1) * V7X_LANES)
            lsl = slice(jj * V7X_LANES, (jj + 1) * V7X_LANES)
            xdt_b = xdt_g[:, lsl].astype(BF16)
            ys = []
            for hh in (h0, h0 + 1):
                seg = acum[:, hh:hh + 1] - acum_t[hh:hh + 1, :]
                dec = jnp.exp(jnp.where(causal, seg, -jnp.inf))
                ys.append(_dot((cb * dec).astype(BF16), xdt_b))
            lane = lax.broadcasted_iota(I32, (Q, V7X_LANES), 1)
            y = jnp.where(lane < head_dim, ys[0], ys[1])
            y = y + y_off[:, lsl] + act_ref[:, sl] * dsk_ref[:, sl]
            zp = z_ref[:, sl]
            yz_ref[:, sl] = y * (zp * _sigmoid(zp))
        xw_g = (xdt_g * exp_ref[DTE, :, gsl]).astype(BF16)
        new_states = _dot(b_f.T.astype(BF16), xw_g)
        state_ref[:, gsl] = s_old * exp_ref[DIN, Q - 1:Q, gsl] + new_states
        yzg = yz_ref[:, g * gw:(g + 1) * gw]
        ms = jnp.mean(yzg * yzg, axis=-1, keepdims=True)
        yn_ref[:, g * gw:(g + 1) * gw] = (
            (yzg * lax.rsqrt(ms + EPS)) * gn_ref[:, g * gw:(g + 1) * gw]).astype(yn_ref.dtype)


def _ssd_scan(xbc, z, dt, cw, cb, a_row, dsk, gn, *, d_inner):
    B, L, cdim = xbc.shape
    Q = SSD_CHUNK
    nd = dt.shape[-1]
    full2 = lambda b, c: (0, 0)
    head_of_col = jnp.arange(d_inner, dtype=I32) // SSD_HEAD_DIM
    e1 = (jnp.arange(nd, dtype=I32)[:, None] == head_of_col[None, :]).astype(BF16)
    e2 = jnp.concatenate([e1, e1], axis=0)
    kern = functools.partial(_ssd_kernel, Q=Q, d_inner=d_inner, n_groups=SSD_N_GROUPS,
                             d_state=SSD_D_STATE, head_dim=SSD_HEAD_DIM)
    return pl.pallas_call(
        kern,
        grid=(B, L // Q),
        in_specs=[
            pl.BlockSpec((None, Q, cdim), lambda b, c: (b, c, 0)),
            pl.BlockSpec((None, Q, d_inner), lambda b, c: (b, c, 0)),
            pl.BlockSpec((None, Q, nd), lambda b, c: (b, c, 0)),
            pl.BlockSpec(cw.shape, full2),
            pl.BlockSpec(cb.shape, full2),
            pl.BlockSpec(a_row.shape, full2),
            pl.BlockSpec(dsk.shape, full2),
            pl.BlockSpec(gn.shape, full2),
            pl.BlockSpec(e2.shape, full2),
        ],
        out_specs=pl.BlockSpec((None, Q, d_inner), lambda b, c: (b, c, 0)),
        out_shape=jax.ShapeDtypeStruct((B, L, d_inner), BF16),
        scratch_shapes=[
            pltpu.VMEM((SSD_D_STATE, d_inner), F32),
            pltpu.VMEM((Q + V7X_SUBLANES, cdim), F32),
            pltpu.VMEM((Q, cdim), F32),
            pltpu.VMEM((Q, d_inner), F32),
            pltpu.VMEM((3, Q, d_inner), F32),
        ],
        compiler_params=_cparams(2),
        name="ssd_scan",
    )(xbc, z, dt, cw, cb, a_row, dsk, gn, e2)


def _proj_res_kernel(a_ref, w_ref, r_ref, o_ref):
    o_ref[...] = r_ref[...] + _dot(a_ref[...], w_ref[...])


def _proj_res(a, w, res, *, tm):
    T, K = a.shape
    N = w.shape[1]
    return pl.pallas_call(
        _proj_res_kernel,
        grid=(T // tm,),
        in_specs=[
            pl.BlockSpec((tm, K), lambda i: (i, 0)),
            pl.BlockSpec((K, N), lambda i: (0, 0)),
            pl.BlockSpec((tm, N), lambda i: (i, 0)),
        ],
        out_specs=pl.BlockSpec((tm, N), lambda i: (i, 0)),
        out_shape=jax.ShapeDtypeStruct((T, N), F32),
        compiler_params=_cparams(1),
        name="proj_residual",
    )(a, w, res)


def _qkv_kernel(x_ref, g_ref, w_ref, q_ref, k_ref, v_ref, *, n_heads, hd, scale):
    xn = _rmsnorm(x_ref[...], g_ref[...]).astype(BF16)
    qkv = _dot(xn, w_ref[...])
    d = n_heads * hd
    for h in range(n_heads):
        q_ref[h] = (qkv[:, h * hd:(h + 1) * hd] * scale).astype(BF16)
        k_ref[h] = qkv[:, d + h * hd:d + (h + 1) * hd].astype(BF16)
        v_ref[h] = qkv[:, 2 * d + h * hd:2 * d + (h + 1) * hd].astype(BF16)


def _qkv_proj(x, g, w, *, n_heads, hd, tm):
    B, L, D = x.shape
    kern = functools.partial(_qkv_kernel, n_heads=n_heads, hd=hd, scale=1.0 / math.sqrt(hd))
    hspec = pl.BlockSpec((None, n_heads, tm, hd), lambda b, i: (b, 0, i, 0))
    hshape = jax.ShapeDtypeStruct((B, n_heads, L, hd), BF16)
    return pl.pallas_call(
        kern,
        grid=(B, L // tm),
        in_specs=[
            pl.BlockSpec((None, tm, D), lambda b, i: (b, i, 0)),
            pl.BlockSpec((1, D), lambda b, i: (0, 0)),
            pl.BlockSpec(w.shape, lambda b, i: (0, 0)),
        ],
        out_specs=[hspec, hspec, hspec],
        out_shape=[hshape, hshape, hshape],
        compiler_params=_cparams(2),
        name="sb_qkv_proj",
    )(x, g, w)


def _attn_kernel(q_ref, k_ref, v_ref, o_ref, acc_ref, carry_ref, *, tq, tk):
    qi = pl.program_id(1)
    nh, _, hd = q_ref.shape
    n_diag = tq // tk
    rows_t = lax.broadcasted_iota(I32, (tk, tk), 0)
    cols_t = lax.broadcasted_iota(I32, (tk, tk), 1)
    tri = jnp.where(rows_t >= cols_t, 1.0, 0.0).astype(BF16)
    tri2 = jnp.concatenate([tri, tri], axis=0)

    def block(k0, r0, diag):
        nr = tq - r0
        if diag:
            before = (lax.broadcasted_iota(I32, (nr, tk), 1) < lax.broadcasted_iota(I32, (nr, tk), 0))
        zs, sps = [], []
        for h in range(nh):
            z = _dot_nt(q_ref[h, r0:tq, :], k_ref[h, pl.ds(k0, tk), :])
            sp = _softplus(z)
            if diag:
                sp = jnp.where(before, sp, 0.0)
            zs.append(z)
            sps.append(sp)
        hi, lo = _split2(jnp.concatenate(sps, axis=0))
        cs_all = _dot(jnp.concatenate([hi, lo], axis=1), tri2)
        for h in range(nh):
            cs = cs_all[h * nr:(h + 1) * nr, :]
            carry = carry_ref[h, r0:tq, 0:1]
            log_a = zs[h] - cs + carry
            if diag:
                log_a = jnp.where(before, log_a, -jnp.inf)
            a = jnp.exp(log_a)
            acc_ref[h, r0:tq, :] += _dot(a.astype(BF16), v_ref[h, pl.ds(k0, tk), :])
            carry_ref[h, r0:tq, :] = jnp.broadcast_to(carry - cs[:, 0:1], (nr, carry_ref.shape[2]))

    acc_ref[...] = jnp.zeros_like(acc_ref)
    carry_ref[...] = jnp.zeros_like(carry_ref)
    q0 = qi * tq
    for d in range(n_diag - 1, -1, -1):
        block(pl.multiple_of(q0 + d * tk, tk), d * tk, True)

    def cond(st):
        kb, live = st
        return jnp.logical_and(kb >= 0, live > ATTN_DEAD_LOG)

    def body(st):
        kb, _ = st
        block(pl.multiple_of(kb * tk, tk), 0, False)
        return kb - 1, jnp.max(carry_ref[:, :, 0:1])

    lax.while_loop(cond, body, (qi * n_diag - 1, jnp.max(carry_ref[:, :, 0:1])))
    for h in range(nh):
        o_ref[:, h * hd:(h + 1) * hd] = acc_ref[h].astype(o_ref.dtype)


def _attention(q, k, v, *, B, n_heads, tq, tk, nh):
    BH, L, hd = q.shape
    kern = functools.partial(_attn_kernel, tq=tq, tk=tk)
    groups = n_heads // nh
    return pl.pallas_call(
        kern,
        grid=(BH // nh, L // tq),
        in_specs=[
            pl.BlockSpec((nh, tq, hd), lambda g, i: (g, i, 0)),
            pl.BlockSpec((nh, L, hd), lambda g, i: (g, 0, 0), pipeline_mode=pl.Buffered(1)),
            pl.BlockSpec((nh, L, hd), lambda g, i: (g, 0, 0), pipeline_mode=pl.Buffered(1)),
        ],
        out_specs=pl.BlockSpec((None, tq, nh * hd), lambda g, i: (g // groups, i, g % groups)),
        out_shape=jax.ShapeDtypeStruct((B, L, n_heads * hd), BF16),
        scratch_shapes=[pltpu.VMEM((nh, tq, hd), F32), pltpu.VMEM((nh, tq, V7X_LANES), F32)],
        compiler_params=_cparams(2),
        name="sb_attention",
    )(q, k, v)


RI_E0, RI_E1, RI_G0, RI_G1, RI_R0, RI_R1 = 0, 1, 2, 3, 4, 5


def _router_kernel(h_ref, g_ref, whm_ref, b_ref, ri_ref, cnt_ref, carry_ref, *, n_groups, n_experts):
    i = pl.program_id(0)
    tm = h_ref.shape[0]
    epg = n_experts // n_groups

    @pl.when(i == 0)
    def _init():
        carry_ref[...] = jnp.zeros_like(carry_ref)

    xn = _rmsnorm(h_ref[...], g_ref[...])
    xh, xm = _split2(xn)
    both = _dot(xh, whm_ref[...])
    logits = (both[:, :V7X_LANES] + (both[:, V7X_LANES:] + _dot(xm, whm_ref[:, :V7X_LANES]))) + b_ref[...]

    lane = lax.broadcasted_iota(I32, (tm, V7X_LANES), 1).astype(F32)
    neg = -jnp.inf
    no_lane = float(V7X_LANES)
    gl = jnp.where(lane < n_groups, logits, neg)
    gmax = jnp.max(gl, axis=-1, keepdims=True)
    gidx = jnp.min(jnp.where(gl == gmax, lane, no_lane), axis=-1, keepdims=True)
    g_w = 1.0 / jnp.sum(jnp.exp(gl - gmax), axis=-1, keepdims=True)

    first = n_groups + gidx * epg
    in_group = jnp.logical_and(lane >= first, lane < first + epg)
    el = jnp.where(in_group, logits, neg)
    m0 = jnp.max(el, axis=-1, keepdims=True)
    i0 = jnp.min(jnp.where(el == m0, lane, no_lane), axis=-1, keepdims=True)
    el1 = jnp.where(lane == i0, neg, el)
    m1 = jnp.max(el1, axis=-1, keepdims=True)
    i1 = jnp.min(jnp.where(el1 == m1, lane, no_lane), axis=-1, keepdims=True)
    d = jnp.exp(m1 - m0)
    p0 = 1.0 / (1.0 + d)
    gate0 = g_w * p0
    gate1 = g_w * (d * p0)

    oh0 = jnp.where(lane == i0, 1.0, 0.0)
    oh1 = jnp.where(lane == i1, 1.0, 0.0)
    r = lax.broadcasted_iota(I32, (tm, tm), 0)
    cc = lax.broadcasted_iota(I32, (tm, tm), 1)
    strict = jnp.where(cc < r, 1.0, 0.0).astype(BF16)
    pre0 = _dot(strict, oh0.astype(BF16))
    pre1 = _dot(strict, oh1.astype(BF16))
    cnt0 = jnp.sum(oh0, axis=0, keepdims=True)
    cnt1 = jnp.sum(oh1, axis=0, keepdims=True)
    base = carry_ref[0:1, :]
    rank0 = jnp.sum(oh0 * (pre0 + base), axis=-1, keepdims=True)
    rank1 = jnp.sum(oh1 * (pre1 + (base + cnt0)), axis=-1, keepdims=True)
    total = base + cnt0 + cnt1
    carry_ref[...] = jnp.broadcast_to(total, carry_ref.shape)
    cnt_ref[...] = jnp.broadcast_to(total, cnt_ref.shape)

    e0 = i0 - n_groups
    e1 = i1 - n_groups
    ri = jnp.where(lane == RI_E0, e0, 0.0)
    ri = jnp.where(lane == RI_E1, e1, ri)
    ri = jnp.where(lane == RI_G0, gate0, ri)
    ri = jnp.where(lane == RI_G1, gate1, ri)
    ri = jnp.where(lane == RI_R0, rank0, ri)
    ri = jnp.where(lane == RI_R1, rank1, ri)
    ri_ref[...] = ri


def _router(h2, g, w_hm, b_r, *, n_experts, tm):
    T, D = h2.shape
    kern = functools.partial(_router_kernel, n_groups=MOE_GROUPS, n_experts=n_experts)
    return pl.pallas_call(
        kern,
        grid=(T // tm,),
        in_specs=[
            pl.BlockSpec((tm, D), lambda i: (i, 0)),
            pl.BlockSpec((1, D), lambda i: (0, 0)),
            pl.BlockSpec((D, 2 * V7X_LANES), lambda i: (0, 0)),
            pl.BlockSpec((1, V7X_LANES), lambda i: (0, 0)),
        ],
        out_specs=[
            pl.BlockSpec((tm, V7X_LANES), lambda i: (i, 0)),
            pl.BlockSpec((V7X_SUBLANES, V7X_LANES), lambda i: (0, 0)),
        ],
        out_shape=[
            jax.ShapeDtypeStruct((T, V7X_LANES), F32),
            jax.ShapeDtypeStruct((V7X_SUBLANES, V7X_LANES), F32),
        ],
        scratch_shapes=[pltpu.VMEM((V7X_SUBLANES, V7X_LANES), F32)],
        compiler_params=_cparams(1),
        name="moe_router",
    )(h2, g, w_hm, b_r)


def _tile_copy(src_ref, src_row, dst_ref, dst_row, sem):
    s = pl.multiple_of(src_row * V7X_SUBLANES, V7X_SUBLANES)
    d = pl.multiple_of(dst_row * V7X_SUBLANES, V7X_SUBLANES)
    return pltpu.make_async_copy(src_ref.at[pl.ds(s, V7X_SUBLANES)], dst_ref.at[pl.ds(d, V7X_SUBLANES)], sem)


def _token_loop(tm, fn):
    def step(t, c):
        fn(t)
        return c

    lax.fori_loop(0, tm, step, 0, unroll=SCALAR_UNROLL)


def _dispatch_kernel(dest_ref, h_ref, g_ref, init_ref, rows_ref, xt_ref, sem):
    del init_ref
    i = pl.program_id(0)
    tm = h_ref.shape[0]
    cur = i % 2
    _tile_rows_store(xt_ref.at[cur], _rmsnorm(h_ref[...], g_ref[...]), tm)

    def issue(t):
        for k in range(MOE_TOP_K):
            _tile_copy(xt_ref.at[cur], t, rows_ref, dest_ref[0, 0, MOE_TOP_K * t + k],
                       sem.at[cur]).start(priority=k)

    def drain(b):
        def wait(t):
            for _ in range(MOE_TOP_K):
                _tile_copy(xt_ref.at[b], t, rows_ref, 0, sem.at[b]).wait()

        _token_loop(tm, wait)

    _token_loop(tm, issue)

    @pl.when(i >= 1)
    def _prev():
        drain(1 - cur)

    @pl.when(i == pl.num_programs(0) - 1)
    def _last():
        drain(cur)


def _dispatch(dest3, h2, g, n_rows, *, tm):
    T, D = h2.shape
    tiles = (n_rows * V7X_SUBLANES, V7X_LANES)
    init = jnp.zeros(tiles, F32)
    return pl.pallas_call(
        _dispatch_kernel,
        grid=(T // tm,),
        in_specs=[
            pl.BlockSpec((1, 1, MOE_TOP_K * tm), lambda i: (i, 0, 0), memory_space=pltpu.SMEM),
            pl.BlockSpec((tm, D), lambda i: (i, 0)),
            pl.BlockSpec((1, D), lambda i: (0, 0)),
            pl.BlockSpec(memory_space=pl.ANY),
        ],
        out_specs=pl.BlockSpec(memory_space=pl.ANY),
        out_shape=jax.ShapeDtypeStruct(tiles, F32),
        scratch_shapes=[pltpu.VMEM((2, tm * V7X_SUBLANES, V7X_LANES), F32), pltpu.SemaphoreType.DMA((2,))],
        input_output_aliases={3: 0},
        compiler_params=_cparams(1),
        name="moe_dispatch",
    )(dest3, h2, g, init)


def _expert_kernel(be_ref, nu_ref, x_ref, wg_ref, wu_ref, wd_ref, y_ref):
    i = pl.program_id(0)
    rb = x_ref.shape[0] // V7X_SUBLANES

    @pl.when(i < nu_ref[0])
    def _live():
        x = _tile_rows_load(x_ref, rb).astype(BF16)
        gt = _dot(x, wg_ref[0].astype(BF16))
        up = _dot(x, wu_ref[0].astype(BF16))
        act = (gt * _sigmoid(gt)) * up
        _tile_rows_store(y_ref, _dot(act.astype(BF16), wd_ref[0].astype(BF16)), rb)

    @pl.when(i >= nu_ref[0])
    def _dead():
        y_ref[...] = jnp.zeros_like(y_ref)


def _experts(blk_expert, n_used, x_rows, w_gate, w_up, w_down, *, layer, rb):
    n_blk = x_rows.shape[0] // (rb * V7X_SUBLANES)
    _, E, D, F = w_gate.shape
    tile_blk = (rb * V7X_SUBLANES, V7X_LANES)

    def row_map(i, be, nu):
        return (jnp.minimum(i, nu[0] - 1), 0)

    def w_map(i, be, nu):
        return (layer, be[i], 0, 0)

    grid_spec = pltpu.PrefetchScalarGridSpec(
        num_scalar_prefetch=2,
        grid=(n_blk,),
        in_specs=[
            pl.BlockSpec(tile_blk, row_map),
            pl.BlockSpec((None, 1, D, F), w_map),
            pl.BlockSpec((None, 1, D, F), w_map),
            pl.BlockSpec((None, 1, F, D), w_map),
        ],
        out_specs=pl.BlockSpec(tile_blk, lambda i, be, nu: (i, 0)),
    )
    return pl.pallas_call(
        _expert_kernel,
        grid_spec=grid_spec,
        out_shape=jax.ShapeDtypeStruct(x_rows.shape, F32),
        compiler_params=_cparams(1),
        name="moe_experts",
    )(blk_expert, n_used, x_rows, w_gate, w_up, w_down)


def _combine_kernel(dest_ref, destn_ref, h_ref, ri_ref, p_ref, yrows_ref, pn_ref, wg_ref, bg_ref, wp_ref,
                    fn_ref, o_ref, ybuf_ref, sem, *, final_norm):
    i = pl.program_id(0)
    tm = h_ref.shape[0]
    cur = i % 2

    def fetch(d_ref, b):
        def issue(t):
            for k in range(MOE_TOP_K):
                _tile_copy(yrows_ref, d_ref[0, 0, MOE_TOP_K * t + k], ybuf_ref.at[b, k], t,
                           sem.at[b]).start(priority=k)

        _token_loop(tm, issue)

    @pl.when(i == 0)
    def _first():
        fetch(dest_ref, 0)

    @pl.when(i + 1 < pl.num_programs(0))
    def _next():
        fetch(destn_ref, 1 - cur)

    def wait(t):
        for k in range(MOE_TOP_K):
            _tile_copy(yrows_ref, 0, ybuf_ref.at[cur, k], t, sem.at[cur]).wait()

    _token_loop(tm, wait)

    ri = ri_ref[...]
    h = h_ref[...] + (_tile_rows_load(ybuf_ref.at[cur, 0], tm) * ri[:, RI_G0:RI_G0 + 1]
                      + _tile_rows_load(ybuf_ref.at[cur, 1], tm) * ri[:, RI_G1:RI_G1 + 1])
    xn = _rmsnorm(h, pn_ref[...]).astype(BF16)
    gate = _sigmoid(_dot(xn, wg_ref[...]) + bg_ref[...])
    h = h + gate * _dot(p_ref[...].astype(BF16), wp_ref[...])
    if final_norm:
        h = _rmsnorm(h, fn_ref[...])
    o_ref[...] = h


def _combine(dest3, h2, ri, p3, y_rows, pn, wg, bg, wp, fnorm, *, layer, tm, final_norm):
    T, D = h2.shape
    P = p3.shape[2]
    kern = functools.partial(_combine_kernel, final_norm=final_norm)
    row = lambda i: (0, 0)
    last = T // tm - 1
    return pl.pallas_call(
        kern,
        grid=(T // tm,),
        in_specs=[
            pl.BlockSpec((1, 1, MOE_TOP_K * tm), lambda i: (i, 0, 0), memory_space=pltpu.SMEM),
            pl.BlockSpec((1, 1, MOE_TOP_K * tm), lambda i: (jnp.minimum(i + 1, last), 0, 0),
                         memory_space=pltpu.SMEM),
            pl.BlockSpec((tm, D), lambda i: (i, 0)),
            pl.BlockSpec((tm, V7X_LANES), lambda i: (i, 0)),
            pl.BlockSpec((None, tm, P), lambda i: (layer, i, 0)),
            pl.BlockSpec(memory_space=pl.ANY),
            pl.BlockSpec((1, D), row),
            pl.BlockSpec((D, D), row),
            pl.BlockSpec((1, D), row),
            pl.BlockSpec((P, D), row),
            pl.BlockSpec((1, D), row),
        ],
        out_specs=pl.BlockSpec((tm, D), lambda i: (i, 0)),
        out_shape=jax.ShapeDtypeStruct((T, D), F32),
        scratch_shapes=[pltpu.VMEM((2, MOE_TOP_K, tm * V7X_SUBLANES, V7X_LANES), F32),
                        pltpu.SemaphoreType.DMA((2,))],
        compiler_params=_cparams(1),
        name="moe_combine_ple",
    )(dest3, dest3, h2, ri, p3, y_rows, pn, wg, bg, wp, fnorm)


TM_PROJ = 256
TM_RES = 512
TM_TOKEN = 256
TM_MOVE = 256
EXPERT_ROWS = 512
ATTN_Q_BLOCK = 512
ATTN_K_BLOCK = 256
ATTN_HEADS = 4

def _row(v):
    return v.reshape(1, -1).astype(F32)


def _moe_ple(h2, p3, moe_norm, w_rg, b_rg, w_re, b_re, w_gate, w_up, w_down,
             ple_norm, ple_w_gate, ple_b_gate, ple_w_proj, final_norm, *, layer, last):
    T, D = h2.shape
    G = w_rg.shape[1]
    E = w_re.shape[1]
    TK = T * MOE_TOP_K
    pad = V7X_LANES - G - E
    w_r = jnp.concatenate([w_rg, w_re, jnp.zeros((D, pad), F32)], axis=1)
    b_r = jnp.concatenate([b_rg, b_re, jnp.zeros((pad,), F32)]).reshape(1, V7X_LANES)
    w_hi = w_r.astype(BF16)
    w_mid = (w_r - w_hi.astype(F32)).astype(BF16)
    ri, cnt = _router(h2, _row(moe_norm), jnp.concatenate([w_hi, w_mid], axis=1), b_r,
                      n_experts=E, tm=TM_TOKEN)

    rb = EXPERT_ROWS
    counts = cnt[0, G:G + E].astype(I32)
    padded = ((counts + rb - 1) // rb) * rb
    pad_end = jnp.cumsum(padded)
    pad_start = pad_end - padded
    n_blk = TK // rb + E
    n_rows = n_blk * rb
    eids = jnp.arange(E, dtype=I32)
    expert = ri[:, RI_E0:RI_E1 + 1].astype(I32)
    rank = ri[:, RI_R0:RI_R1 + 1].astype(I32)
    dest = jnp.sum(jnp.where(expert[:, :, None] == eids, pad_start, 0), axis=-1) + rank
    tmv = min(TM_MOVE, T)
    dest3 = dest.reshape(T // tmv, 1, MOE_TOP_K * tmv)
    blk_row0 = jnp.arange(n_blk, dtype=I32) * rb
    blk_expert = jnp.minimum(jnp.sum((pad_end[None, :] <= blk_row0[:, None]).astype(I32), axis=1), E - 1)
    n_used = (pad_end[-1:] // rb).astype(I32)

    x_rows = _dispatch(dest3, h2, _row(moe_norm), n_rows, tm=tmv)
    y_rows = _experts(blk_expert, n_used, x_rows, w_gate, w_up, w_down, layer=layer, rb=rb)
    return _combine(dest3, h2, ri, p3, y_rows, _row(ple_norm), ple_w_gate.astype(BF16), _row(ple_b_gate),
                    ple_w_proj.astype(BF16), _row(final_norm), layer=layer, tm=tmv, final_norm=last)


def _ssd_layer(h2, B, L, norm, w_in, conv_w, conv_b, dt_bias, a_log, d_skip, gnorm, w_out):
    T, D = h2.shape
    H = a_log.shape[0]
    d_inner = H * SSD_HEAD_DIM
    cdim = conv_w.shape[1]
    wz = w_in[:, :d_inner].astype(BF16)
    wx = w_in[:, d_inner:d_inner + cdim].astype(BF16)
    padh = V7X_LANES - H
    wdt = jnp.concatenate([w_in[:, d_inner + cdim:], jnp.zeros((D, padh), F32)], axis=1).astype(BF16)
    dtb = jnp.concatenate([dt_bias, jnp.zeros((padh,), F32)]).reshape(1, V7X_LANES)
    a_row = jnp.concatenate([-jnp.exp(a_log.astype(F32)), jnp.zeros((padh,), F32)]).reshape(1, V7X_LANES)
    dsk = jnp.repeat(d_skip.astype(F32), SSD_HEAD_DIM).reshape(1, d_inner)

    z, xbc, dt = _in_proj(h2, _row(norm), wz, wx, wdt, dtb, tm=TM_PROJ)
    yn = _ssd_scan(xbc.reshape(B, L, cdim), z.reshape(B, L, d_inner), dt.reshape(B, L, V7X_LANES),
                   conv_w.astype(F32), _row(conv_b), a_row, dsk, _row(gnorm), d_inner=d_inner)
    return _proj_res(yn.reshape(T, d_inner), w_out.astype(BF16), h2, tm=TM_RES)


def _sb_layer(h2, B, L, norm, w_qkv, w_o):
    T, D = h2.shape
    hd = SB_HEAD_DIM
    n_heads = w_o.shape[0] // hd
    q, k, v = _qkv_proj(h2.reshape(B, L, D), _row(norm), w_qkv.astype(BF16),
                        n_heads=n_heads, hd=hd, tm=TM_PROJ)
    bh = B * n_heads
    o = _attention(q.reshape(bh, L, hd), k.reshape(bh, L, hd), v.reshape(bh, L, hd),
                   B=B, n_heads=n_heads, tq=min(ATTN_Q_BLOCK, L), tk=ATTN_K_BLOCK, nh=ATTN_HEADS)
    return _proj_res(o.reshape(T, n_heads * hd), w_o.astype(BF16), h2, tm=TM_RES)


def kernel(x, p, ssd_norm, ssd_w_in, ssd_conv_w, ssd_conv_b, ssd_dt_bias, ssd_a_log, ssd_d, ssd_gnorm, ssd_w_out, sb_norm, sb_w_qkv, sb_w_o, moe_norm, moe_w_rg, moe_b_rg, moe_w_re, moe_b_re, moe_w_gate, moe_w_up, moe_w_down, ple_norm, ple_w_gate, ple_b_gate, ple_w_proj, final_norm):
    B, L, D = x.shape
    depth = p.shape[0]
    T = B * L
    n_mixers = 2
    h = x.reshape(T, D)
    p3 = p.reshape(depth, T, p.shape[-1])
    for i in range(depth):
        j = i // n_mixers
        if i % n_mixers == 0:
            h = _ssd_layer(h, B, L, ssd_norm[j], ssd_w_in[j], ssd_conv_w[j], ssd_conv_b[j], ssd_dt_bias[j],
                           ssd_a_log[j], ssd_d[j], ssd_gnorm[j], ssd_w_out[j])
        else:
            h = _sb_layer(h, B, L, sb_norm[j], sb_w_qkv[j], sb_w_o[j])
        h = _moe_ple(h, p3, moe_norm[i], moe_w_rg[i], moe_b_rg[i], moe_w_re[i], moe_b_re[i],
                     moe_w_gate, moe_w_up, moe_w_down, ple_norm[i], ple_w_gate[i], ple_b_gate[i],
                     ple_w_proj[i], final_norm, layer=i, last=(i == depth - 1))
    return h.reshape(B, L, D)
```

```python
import functools
import math

import jax
import jax.numpy as jnp
from jax import lax
from jax.experimental import pallas as pl
from jax.experimental.pallas import tpu as pltpu

F32 = jnp.float32
BF16 = jnp.bfloat16
I32 = jnp.int32

EPS = 1e-6
V7X_LANES = 128
V7X_SUBLANES = 8
V7X_VMEM_LIMIT_BYTES = 56 * 1024 * 1024

SSD_HEAD_DIM = 64
SSD_N_GROUPS = 4
SSD_D_STATE = 128
SSD_CHUNK = 128
SSD_CONV = 4
SB_HEAD_DIM = 128
MOE_GROUPS = 8
MOE_TOP_K = 2

SCALAR_UNROLL = 8

ATTN_DEAD_LOG = -110.0


def _cparams(n_axes):
    return pltpu.CompilerParams(
        dimension_semantics=("arbitrary",) * n_axes,
        vmem_limit_bytes=V7X_VMEM_LIMIT_BYTES,
    )


def _dot(a, b):
    return jnp.dot(a, b, preferred_element_type=F32)


def _dot_nt(a, b):
    return lax.dot_general(a, b, (((1,), (1,)), ((), ())), preferred_element_type=F32)


def _split3(x):
    hi = x.astype(BF16)
    r1 = x - hi.astype(F32)
    mid = r1.astype(BF16)
    lo = (r1 - mid.astype(F32)).astype(BF16)
    return hi, mid, lo


def _split2(x):
    hi = x.astype(BF16)
    lo = (x - hi.astype(F32)).astype(BF16)
    return hi, lo


def _dot_exact_lhs(a_bf16, x):
    hi, mid, lo = _split3(x)
    return _dot(a_bf16, hi) + _dot(a_bf16, mid) + _dot(a_bf16, lo)


def _sigmoid(x):
    return 1.0 / (1.0 + jnp.exp(-x))


def _softplus(x):
    return jnp.maximum(x, 0.0) + jnp.log(1.0 + jnp.exp(-jnp.abs(x)))


def _rmsnorm(x, g):
    ms = jnp.mean(x * x, axis=-1, keepdims=True)
    return (x * lax.rsqrt(ms + EPS)) * g


def _tile_rows_load(ref, n):
    return jnp.concatenate(
        [ref[pl.ds(j, n, stride=V7X_SUBLANES), :] for j in range(V7X_SUBLANES)], axis=1)


def _tile_rows_store(ref, val, n):
    for j in range(V7X_SUBLANES):
        ref[pl.ds(j, n, stride=V7X_SUBLANES), :] = val[:, j * V7X_LANES:(j + 1) * V7X_LANES]


PACKED_SUBLANES = 4
_HIGH_HALF = 0xFFFF0000


def _packed_rows_store(ref, val, n):
    half = val.shape[1] // 2
    bits = lax.bitcast_convert_type(val.astype(BF16).astype(F32), jnp.uint32)
    words = bits[:, :half] | (bits[:, half:] >> 16)
    for j in range(PACKED_SUBLANES):
        ref[pl.ds(j, n, stride=PACKED_SUBLANES), :] = words[:, j * V7X_LANES:(j + 1) * V7X_LANES]


def _packed_rows_load(ref, n):
    words = jnp.concatenate(
        [ref[pl.ds(j, n, stride=PACKED_SUBLANES), :] for j in range(PACKED_SUBLANES)], axis=1)
    first = lax.bitcast_convert_type(words & jnp.uint32(_HIGH_HALF), F32)
    second = lax.bitcast_convert_type(words << 16, F32)
    return jnp.concatenate([first, second], axis=1).astype(BF16)


def _in_proj_kernel(x_ref, g_ref, wz_ref, wx_ref, wdt_ref, dtb_ref, z_ref, xbc_ref, dt_ref):
    xn = _rmsnorm(x_ref[...], g_ref[...]).astype(BF16)
    z_ref[...] = _dot(xn, wz_ref[...])
    xbc_ref[...] = _dot(xn, wx_ref[...])
    dt_ref[...] = _softplus(_dot(xn, wdt_ref[...]) + dtb_ref[...])


def _in_proj(x2, g, wz, wx, wdt, dtb, *, tm):
    T, D = x2.shape
    nz, nx, nd = wz.shape[1], wx.shape[1], wdt.shape[1]
    full = lambda i: (0, 0)
    return pl.pallas_call(
        _in_proj_kernel,
        grid=(T // tm,),
        in_specs=[
            pl.BlockSpec((tm, D), lambda i: (i, 0)),
            pl.BlockSpec((1, D), full),
            pl.BlockSpec((D, nz), full),
            pl.BlockSpec((D, nx), full),
            pl.BlockSpec((D, nd), full),
            pl.BlockSpec((1, nd), full),
        ],
        out_specs=[
            pl.BlockSpec((tm, nz), lambda i: (i, 0)),
            pl.BlockSpec((tm, nx), lambda i: (i, 0)),
            pl.BlockSpec((tm, nd), lambda i: (i, 0)),
        ],
        out_shape=[
            jax.ShapeDtypeStruct((T, nz), F32),
            jax.ShapeDtypeStruct((T, nx), F32),
            jax.ShapeDtypeStruct((T, nd), F32),
        ],
        compiler_params=_cparams(1),
        name="ssd_in_proj",
    )(x2, g, wz, wx, wdt, dtb)


def _ssd_kernel(xbc_ref, z_ref, dt_ref, cw_ref, cb_ref, a_ref, dsk_ref, gn_ref, e2_ref, yn_ref,
                state_ref, cbuf_ref, act_ref, yz_ref, exp_ref, *, Q, d_inner, n_groups, d_state, head_dim):
    c = pl.program_id(1)
    N = d_state
    gw = d_inner // n_groups
    pairs_per_group = gw // V7X_LANES
    halo = V7X_SUBLANES

    @pl.when(c == 0)
    def _init():
        state_ref[...] = jnp.zeros_like(state_ref)
        cbuf_ref[0:halo, :] = jnp.zeros((halo, cbuf_ref.shape[1]), F32)

    x_in = xbc_ref[...]
    cbuf_ref[halo:halo + Q, :] = x_in
    cw = cw_ref[...]
    conv = x_in * cw[SSD_CONV - 1:SSD_CONV, :] + cb_ref[...]
    for k in range(SSD_CONV - 1):
        off = halo - (SSD_CONV - 1) + k
        conv = conv + cbuf_ref[off:off + Q, :] * cw[k:k + 1, :]
    cbuf_ref[0:halo, :] = x_in[Q - halo:Q, :]
    act_ref[...] = conv * _sigmoid(conv)

    dt = dt_ref[...]
    rows = lax.broadcasted_iota(I32, (Q, Q), 0)
    cols = lax.broadcasted_iota(I32, (Q, Q), 1)
    causal = rows >= cols
    tril = jnp.where(causal, 1.0, 0.0).astype(BF16)
    acum = _dot_exact_lhs(tril, dt * a_ref[...])
    acum_t = acum.T
    a_last = acum[Q - 1:Q, :]
    e2 = e2_ref[...]
    for idx, v in enumerate((dt, jnp.exp(acum), jnp.exp(a_last - acum))):
        hi, lo = _split2(v)
        exp_ref[idx] = _dot(jnp.concatenate([hi, lo], axis=1), e2)
    DT, DIN, DTE = 0, 1, 2

    for g in range(n_groups):
        b_f = act_ref[:, d_inner + g * N:d_inner + (g + 1) * N]
        b_g = b_f.astype(BF16)
        c_g = act_ref[:, d_inner + n_groups * N + g * N:d_inner + n_groups * N + (g + 1) * N].astype(BF16)
        cb = _dot_nt(c_g, b_g)
        gsl = slice(g * gw, (g + 1) * gw)
        s_old = state_ref[:, gsl]
        y_off = _dot(c_g, s_old.astype(BF16)) * exp_ref[DIN, :, gsl]
        xdt_g = act_ref[:, gsl] * exp_ref[DT, :, gsl]
        for jj in range(pairs_per_group):
            j = g * pairs_per_group + jj
            h0 = 2 * j
            sl = slice(j * V7X_LANES, (j + 1) * V7X_LANES)
            lsl = slice(jj * V7X_LANES, (jj + 1) * V7X_LANES)
            xdt_b = xdt_g[:, lsl].astype(BF16)
            ys = []
            for hh in (h0, h0 + 1):
                seg = acum[:, hh:hh + 1] - acum_t[hh:hh + 1, :]
                dec = jnp.exp(jnp.where(causal, seg, -jnp.inf))
                ys.append(_dot((cb * dec).astype(BF16), xdt_b))
            lane = lax.broadcasted_iota(I32, (Q, V7X_LANES), 1)
            y = jnp.where(lane < head_dim, ys[0], ys[1])
            y = y + y_off[:, lsl] + act_ref[:, sl] * dsk_ref[:, sl]
            zp = z_ref[:, sl]
            yz_ref[:, sl] = y * (zp * _sigmoid(zp))
        xw_g = (xdt_g * exp_ref[DTE, :, gsl]).astype(BF16)
        new_states = _dot(b_f.T.astype(BF16), xw_g)
        state_ref[:, gsl] = s_old * exp_ref[DIN, Q - 1:Q, gsl] + new_states
        yzg = yz_ref[:, g * gw:(g + 1) * gw]
        ms = jnp.mean(yzg * yzg, axis=-1, keepdims=True)
        yn_ref[:, g * gw:(g + 1) * gw] = (
            (yzg * lax.rsqrt(ms + EPS)) * gn_ref[:, g * gw:(g + 1) * gw]).astype(yn_ref.dtype)


def _ssd_scan(xbc, z, dt, cw, cb, a_row, dsk, gn, *, d_inner):
    B, L, cdim = xbc.shape
    Q = SSD_CHUNK
    nd = dt.shape[-1]
    full2 = lambda b, c: (0, 0)
    head_of_col = jnp.arange(d_inner, dtype=I32) // SSD_HEAD_DIM
    e1 = (jnp.arange(nd, dtype=I32)[:, None] == head_of_col[None, :]).astype(BF16)
    e2 = jnp.concatenate([e1, e1], axis=0)
    kern = functools.partial(_ssd_kernel, Q=Q, d_inner=d_inner, n_groups=SSD_N_GROUPS,
                             d_state=SSD_D_STATE, head_dim=SSD_HEAD_DIM)
    return pl.pallas_call(
        kern,
        grid=(B, L // Q),
        in_specs=[
            pl.BlockSpec((None, Q, cdim), lambda b, c: (b, c, 0)),
            pl.BlockSpec((None, Q, d_inner), lambda b, c: (b, c, 0)),
            pl.BlockSpec((None, Q, nd), lambda b, c: (b, c, 0)),
            pl.BlockSpec(cw.shape, full2),
            pl.BlockSpec(cb.shape, full2),
            pl.BlockSpec(a_row.shape, full2),
            pl.BlockSpec(dsk.shape, full2),
            pl.BlockSpec(gn.shape, full2),
            pl.BlockSpec(e2.shape, full2),
        ],
        out_specs=pl.BlockSpec((None, Q, d_inner), lambda b, c: (b, c, 0)),
        out_shape=jax.ShapeDtypeStruct((B, L, d_inner), BF16),
        scratch_shapes=[
            pltpu.VMEM((SSD_D_STATE, d_inner), F32),
            pltpu.VMEM((Q + V7X_SUBLANES, cdim), F32),
            pltpu.VMEM((Q, cdim), F32),
            pltpu.VMEM((Q, d_inner), F32),
            pltpu.VMEM((3, Q, d_inner), F32),
        ],
        compiler_params=_cparams(2),
        name="ssd_scan",
    )(xbc, z, dt, cw, cb, a_row, dsk, gn, e2)


def _proj_res_kernel(a_ref, w_ref, r_ref, o_ref):
    o_ref[...] = r_ref[...] + _dot(a_ref[...], w_ref[...])


def _proj_res(a, w, res, *, tm):
    T, K = a.shape
    N = w.shape[1]
    return pl.pallas_call(
        _proj_res_kernel,
        grid=(T // tm,),
        in_specs=[
            pl.BlockSpec((tm, K), lambda i: (i, 0)),
            pl.BlockSpec((K, N), lambda i: (0, 0)),
            pl.BlockSpec((tm, N), lambda i: (i, 0)),
        ],
        out_specs=pl.BlockSpec((tm, N), lambda i: (i, 0)),
        out_shape=jax.ShapeDtypeStruct((T, N), F32),
        compiler_params=_cparams(1),
        name="proj_residual",
    )(a, w, res)


def _qkv_kernel(x_ref, g_ref, w_ref, q_ref, k_ref, v_ref, *, n_heads, hd, scale):
    xn = _rmsnorm(x_ref[...], g_ref[...]).astype(BF16)
    qkv = _dot(xn, w_ref[...])
    d = n_heads * hd
    for h in range(n_heads):
        q_ref[h] = (qkv[:, h * hd:(h + 1) * hd] * scale).astype(BF16)
        k_ref[h] = qkv[:, d + h * hd:d + (h + 1) * hd].astype(BF16)
        v_ref[h] = qkv[:, 2 * d + h * hd:2 * d + (h + 1) * hd].astype(BF16)


def _qkv_proj(x, g, w, *, n_heads, hd, tm):
    B, L, D = x.shape
    kern = functools.partial(_qkv_kernel, n_heads=n_heads, hd=hd, scale=1.0 / math.sqrt(hd))
    hspec = pl.BlockSpec((None, n_heads, tm, hd), lambda b, i: (b, 0, i, 0))
    hshape = jax.ShapeDtypeStruct((B, n_heads, L, hd), BF16)
    return pl.pallas_call(
        kern,
        grid=(B, L // tm),
        in_specs=[
            pl.BlockSpec((None, tm, D), lambda b, i: (b, i, 0)),
            pl.BlockSpec((1, D), lambda b, i: (0, 0)),
            pl.BlockSpec(w.shape, lambda b, i: (0, 0)),
        ],
        out_specs=[hspec, hspec, hspec],
        out_shape=[hshape, hshape, hshape],
        compiler_params=_cparams(2),
        name="sb_qkv_proj",
    )(x, g, w)


def _attn_kernel(q_ref, k_ref, v_ref, o_ref, acc_ref, carry_ref, *, tq, tk):
    qi = pl.program_id(1)
    nh, _, hd = q_ref.shape
    n_diag = tq // tk
    rows_t = lax.broadcasted_iota(I32, (tk, tk), 0)
    cols_t = lax.broadcasted_iota(I32, (tk, tk), 1)
    tri = jnp.where(rows_t >= cols_t, 1.0, 0.0).astype(BF16)
    tri2 = jnp.concatenate([tri, tri], axis=0)

    def block(k0, r0, diag):
        nr = tq - r0
        if diag:
            before = (lax.broadcasted_iota(I32, (nr, tk), 1) < lax.broadcasted_iota(I32, (nr, tk), 0))
        zs, sps = [], []
        for h in range(nh):
            z = _dot_nt(q_ref[h, r0:tq, :], k_ref[h, pl.ds(k0, tk), :])
            sp = _softplus(z)
            if diag:
                sp = jnp.where(before, sp, 0.0)
            zs.append(z)
            sps.append(sp)
        hi, lo = _split2(jnp.concatenate(sps, axis=0))
        cs_all = _dot(jnp.concatenate([hi, lo], axis=1), tri2)
        for h in range(nh):
            cs = cs_all[h * nr:(h + 1) * nr, :]
            carry = carry_ref[h, r0:tq, 0:1]
            log_a = zs[h] - cs + carry
            if diag:
                log_a = jnp.where(before, log_a, -jnp.inf)
            a = jnp.exp(log_a)
            acc_ref[h, r0:tq, :] += _dot(a.astype(BF16), v_ref[h, pl.ds(k0, tk), :])
            carry_ref[h, r0:tq, :] = jnp.broadcast_to(carry - cs[:, 0:1], (nr, carry_ref.shape[2]))

    acc_ref[...] = jnp.zeros_like(acc_ref)
    carry_ref[...] = jnp.zeros_like(carry_ref)
    q0 = qi * tq
    for d in range(n_diag - 1, -1, -1):
        block(pl.multiple_of(q0 + d * tk, tk), d * tk, True)

    def cond(st):
        kb, live = st
        return jnp.logical_and(kb >= 0, live > ATTN_DEAD_LOG)

    def body(st):
        kb, _ = st
        block(pl.multiple_of(kb * tk, tk), 0, False)
        return kb - 1, jnp.max(carry_ref[:, :, 0:1])

    lax.while_loop(cond, body, (qi * n_diag - 1, jnp.max(carry_ref[:, :, 0:1])))
    for h in range(nh):
        o_ref[:, h * hd:(h + 1) * hd] = acc_ref[h].astype(o_ref.dtype)


def _attention(q, k, v, *, B, n_heads, tq, tk, nh):
    BH, L, hd = q.shape
    kern = functools.partial(_attn_kernel, tq=tq, tk=tk)
    groups = n_heads // nh
    return pl.pallas_call(
        kern,
        grid=(BH // nh, L // tq),
        in_specs=[
            pl.BlockSpec((nh, tq, hd), lambda g, i: (g, i, 0)),
            pl.BlockSpec((nh, L, hd), lambda g, i: (g, 0, 0), pipeline_mode=pl.Buffered(1)),
            pl.BlockSpec((nh, L, hd), lambda g, i: (g, 0, 0), pipeline_mode=pl.Buffered(1)),
        ],
        out_specs=pl.BlockSpec((None, tq, nh * hd), lambda g, i: (g // groups, i, g % groups)),
        out_shape=jax.ShapeDtypeStruct((B, L, n_heads * hd), BF16),
        scratch_shapes=[pltpu.VMEM((nh, tq, hd), F32), pltpu.VMEM((nh, tq, V7X_LANES), F32)],
        compiler_params=_cparams(2),
        name="sb_attention",
    )(q, k, v)


RI_E0, RI_E1, RI_G0, RI_G1, RI_R0, RI_R1 = 0, 1, 2, 3, 4, 5


def _router_kernel(h_ref, g_ref, whm_ref, b_ref, ri_ref, cnt_ref, carry_ref, *, n_groups, n_experts):
    i = pl.program_id(0)
    tm = h_ref.shape[0]
    epg = n_experts // n_groups

    @pl.when(i == 0)
    def _init():
        carry_ref[...] = jnp.zeros_like(carry_ref)

    xn = _rmsnorm(h_ref[...], g_ref[...])
    xh, xm = _split2(xn)
    both = _dot(xh, whm_ref[...])
    logits = (both[:, :V7X_LANES] + (both[:, V7X_LANES:] + _dot(xm, whm_ref[:, :V7X_LANES]))) + b_ref[...]

    lane = lax.broadcasted_iota(I32, (tm, V7X_LANES), 1).astype(F32)
    neg = -jnp.inf
    no_lane = float(V7X_LANES)
    gl = jnp.where(lane < n_groups, logits, neg)
    gmax = jnp.max(gl, axis=-1, keepdims=True)
    gidx = jnp.min(jnp.where(gl == gmax, lane, no_lane), axis=-1, keepdims=True)
    g_w = 1.0 / jnp.sum(jnp.exp(gl - gmax), axis=-1, keepdims=True)

    first = n_groups + gidx * epg
    in_group = jnp.logical_and(lane >= first, lane < first + epg)
    el = jnp.where(in_group, logits, neg)
    m0 = jnp.max(el, axis=-1, keepdims=True)
    i0 = jnp.min(jnp.where(el == m0, lane, no_lane), axis=-1, keepdims=True)
    el1 = jnp.where(lane == i0, neg, el)
    m1 = jnp.max(el1, axis=-1, keepdims=True)
    i1 = jnp.min(jnp.where(el1 == m1, lane, no_lane), axis=-1, keepdims=True)
    d = jnp.exp(m1 - m0)
    p0 = 1.0 / (1.0 + d)
    gate0 = g_w * p0
    gate1 = g_w * (d * p0)

    oh0 = jnp.where(lane == i0, 1.0, 0.0)
    oh1 = jnp.where(lane == i1, 1.0, 0.0)
    r = lax.broadcasted_iota(I32, (tm, tm), 0)
    cc = lax.broadcasted_iota(I32, (tm, tm), 1)
    strict = jnp.where(cc < r, 1.0, 0.0).astype(BF16)
    pre0 = _dot(strict, oh0.astype(BF16))
    pre1 = _dot(strict, oh1.astype(BF16))
    cnt0 = jnp.sum(oh0, axis=0, keepdims=True)
    cnt1 = jnp.sum(oh1, axis=0, keepdims=True)
    base = carry_ref[0:1, :]
    rank0 = jnp.sum(oh0 * (pre0 + base), axis=-1, keepdims=True)
    rank1 = jnp.sum(oh1 * (pre1 + (base + cnt0)), axis=-1, keepdims=True)
    total = base + cnt0 + cnt1
    carry_ref[...] = jnp.broadcast_to(total, carry_ref.shape)
    cnt_ref[...] = jnp.broadcast_to(total, cnt_ref.shape)

    e0 = i0 - n_groups
    e1 = i1 - n_groups
    ri = jnp.where(lane == RI_E0, e0, 0.0)
    ri = jnp.where(lane == RI_E1, e1, ri)
    ri = jnp.where(lane == RI_G0, gate0, ri)
    ri = jnp.where(lane == RI_G1, gate1, ri)
    ri = jnp.where(lane == RI_R0, rank0, ri)
    ri = jnp.where(lane == RI_R1, rank1, ri)
    ri_ref[...] = ri


def _router(h2, g, w_hm, b_r, *, n_experts, tm):
    T, D = h2.shape
    kern = functools.partial(_router_kernel, n_groups=MOE_GROUPS, n_experts=n_experts)
    return pl.pallas_call(
        kern,
        grid=(T // tm,),
        in_specs=[
            pl.BlockSpec((tm, D), lambda i: (i, 0)),
            pl.BlockSpec((1, D), lambda i: (0, 0)),
            pl.BlockSpec((D, 2 * V7X_LANES), lambda i: (0, 0)),
            pl.BlockSpec((1, V7X_LANES), lambda i: (0, 0)),
        ],
        out_specs=[
            pl.BlockSpec((tm, V7X_LANES), lambda i: (i, 0)),
            pl.BlockSpec((V7X_SUBLANES, V7X_LANES), lambda i: (0, 0)),
        ],
        out_shape=[
            jax.ShapeDtypeStruct((T, V7X_LANES), F32),
            jax.ShapeDtypeStruct((V7X_SUBLANES, V7X_LANES), F32),
        ],
        scratch_shapes=[pltpu.VMEM((V7X_SUBLANES, V7X_LANES), F32)],
        compiler_params=_cparams(1),
        name="moe_router",
    )(h2, g, w_hm, b_r)


def _tile_copy(src_ref, src_row, dst_ref, dst_row, sem, sub=V7X_SUBLANES):
    s = pl.multiple_of(src_row * sub, sub)
    d = pl.multiple_of(dst_row * sub, sub)
    return pltpu.make_async_copy(src_ref.at[pl.ds(s, sub)], dst_ref.at[pl.ds(d, sub)], sem)


def _token_loop(tm, fn):
    def step(t, c):
        fn(t)
        return c

    lax.fori_loop(0, tm, step, 0, unroll=SCALAR_UNROLL)


def _dispatch_kernel(dest_ref, h_ref, g_ref, init_ref, rows_ref, xt_ref, sem):
    del init_ref
    i = pl.program_id(0)
    tm = h_ref.shape[0]
    cur = i % 2
    _packed_rows_store(xt_ref.at[cur], _rmsnorm(h_ref[...], g_ref[...]), tm)

    def issue(t):
        for k in range(MOE_TOP_K):
            _tile_copy(xt_ref.at[cur], t, rows_ref, dest_ref[0, 0, MOE_TOP_K * t + k],
                       sem.at[cur], PACKED_SUBLANES).start(priority=k)

    def drain(b):
        def wait(t):
            for _ in range(MOE_TOP_K):
                _tile_copy(xt_ref.at[b], t, rows_ref, 0, sem.at[b], PACKED_SUBLANES).wait()

        _token_loop(tm, wait)

    _token_loop(tm, issue)

    @pl.when(i >= 1)
    def _prev():
        drain(1 - cur)

    @pl.when(i == pl.num_programs(0) - 1)
    def _last():
        drain(cur)


def _dispatch(dest3, h2, g, n_rows, *, tm):
    T, D = h2.shape
    assert D == 2 * PACKED_SUBLANES * V7X_LANES
    tiles = (n_rows * PACKED_SUBLANES, V7X_LANES)
    init = jnp.zeros(tiles, jnp.uint32)
    return pl.pallas_call(
        _dispatch_kernel,
        grid=(T // tm,),
        in_specs=[
            pl.BlockSpec((1, 1, MOE_TOP_K * tm), lambda i: (i, 0, 0), memory_space=pltpu.SMEM),
            pl.BlockSpec((tm, D), lambda i: (i, 0)),
            pl.BlockSpec((1, D), lambda i: (0, 0)),
            pl.BlockSpec(memory_space=pl.ANY),
        ],
        out_specs=pl.BlockSpec(memory_space=pl.ANY),
        out_shape=jax.ShapeDtypeStruct(tiles, jnp.uint32),
        scratch_shapes=[pltpu.VMEM((2, tm * PACKED_SUBLANES, V7X_LANES), jnp.uint32),
                        pltpu.SemaphoreType.DMA((2,))],
        input_output_aliases={3: 0},
        compiler_params=_cparams(1),
        name="moe_dispatch",
    )(dest3, h2, g, init)


def _expert_kernel(be_ref, nu_ref, x_ref, wg_ref, wu_ref, wd_ref, y_ref):
    i = pl.program_id(0)
    rb = x_ref.shape[0] // PACKED_SUBLANES

    @pl.when(i < nu_ref[0])
    def _live():
        x = _packed_rows_load(x_ref, rb)
        gt = _dot(x, wg_ref[0].astype(BF16))
        up = _dot(x, wu_ref[0].astype(BF16))
        act = (gt * _sigmoid(gt)) * up
        _tile_rows_store(y_ref, _dot(act.astype(BF16), wd_ref[0].astype(BF16)), rb)

    @pl.when(i >= nu_ref[0])
    def _dead():
        y_ref[...] = jnp.zeros_like(y_ref)


def _experts(blk_expert, n_used, x_rows, w_gate, w_up, w_down, *, layer, rb):
    n_blk = x_rows.shape[0] // (rb * PACKED_SUBLANES)
    _, E, D, F = w_gate.shape
    in_blk = (rb * PACKED_SUBLANES, V7X_LANES)
    tile_blk = (rb * V7X_SUBLANES, V7X_LANES)

    def row_map(i, be, nu):
        return (jnp.minimum(i, nu[0] - 1), 0)

    def w_map(i, be, nu):
        return (layer, be[i], 0, 0)

    grid_spec = pltpu.PrefetchScalarGridSpec(
        num_scalar_prefetch=2,
        grid=(n_blk,),
        in_specs=[
            pl.BlockSpec(in_blk, row_map),
            pl.BlockSpec((None, 1, D, F), w_map),
            pl.BlockSpec((None, 1, D, F), w_map),
            pl.BlockSpec((None, 1, F, D), w_map),
        ],
        out_specs=pl.BlockSpec(tile_blk, lambda i, be, nu: (i, 0)),
    )
    return pl.pallas_call(
        _expert_kernel,
        grid_spec=grid_spec,
        out_shape=jax.ShapeDtypeStruct((n_blk * rb * V7X_SUBLANES, V7X_LANES), F32),
        compiler_params=_cparams(1),
        name="moe_experts",
    )(blk_expert, n_used, x_rows, w_gate, w_up, w_down)


def _combine_kernel(dest_ref, destn_ref, h_ref, ri_ref, p_ref, yrows_ref, pn_ref, wg_ref, bg_ref, wp_ref,
                    fn_ref, o_ref, ybuf_ref, sem, *, final_norm):
    i = pl.program_id(0)
    tm = h_ref.shape[0]
    cur = i % 2

    def fetch(d_ref, b):
        def issue(t):
            for k in range(MOE_TOP_K):
                _tile_copy(yrows_ref, d_ref[0, 0, MOE_TOP_K * t + k], ybuf_ref.at[b, k], t,
                           sem.at[b]).start(priority=k)

        _token_loop(tm, issue)

    @pl.when(i == 0)
    def _first():
        fetch(dest_ref, 0)

    @pl.when(i + 1 < pl.num_programs(0))
    def _next():
        fetch(destn_ref, 1 - cur)

    def wait(t):
        for k in range(MOE_TOP_K):
            _tile_copy(yrows_ref, 0, ybuf_ref.at[cur, k], t, sem.at[cur]).wait()

    _token_loop(tm, wait)

    ri = ri_ref[...]
    h = h_ref[...] + (_tile_rows_load(ybuf_ref.at[cur, 0], tm) * ri[:, RI_G0:RI_G0 + 1]
                      + _tile_rows_load(ybuf_ref.at[cur, 1], tm) * ri[:, RI_G1:RI_G1 + 1])
    xn = _rmsnorm(h, pn_ref[...]).astype(BF16)
    gate = _sigmoid(_dot(xn, wg_ref[...]) + bg_ref[...])
    h = h + gate * _dot(p_ref[...].astype(BF16), wp_ref[...])
    if final_norm:
        h = _rmsnorm(h, fn_ref[...])
    o_ref[...] = h


def _combine(dest3, h2, ri, p3, y_rows, pn, wg, bg, wp, fnorm, *, layer, tm, final_norm):
    T, D = h2.shape
    P = p3.shape[2]
    kern = functools.partial(_combine_kernel, final_norm=final_norm)
    row = lambda i: (0, 0)
    last = T // tm - 1
    return pl.pallas_call(
        kern,
        grid=(T // tm,),
        in_specs=[
            pl.BlockSpec((1, 1, MOE_TOP_K * tm), lambda i: (i, 0, 0), memory_space=pltpu.SMEM),
            pl.BlockSpec((1, 1, MOE_TOP_K * tm), lambda i: (jnp.minimum(i + 1, last), 0, 0),
                         memory_space=pltpu.SMEM),
            pl.BlockSpec((tm, D), lambda i: (i, 0)),
            pl.BlockSpec((tm, V7X_LANES), lambda i: (i, 0)),
            pl.BlockSpec((None, tm, P), lambda i: (layer, i, 0)),
            pl.BlockSpec(memory_space=pl.ANY),
            pl.BlockSpec((1, D), row),
            pl.BlockSpec((D, D), row),
            pl.BlockSpec((1, D), row),
            pl.BlockSpec((P, D), row),
            pl.BlockSpec((1, D), row),
        ],
        out_specs=pl.BlockSpec((tm, D), lambda i: (i, 0)),
        out_shape=jax.ShapeDtypeStruct((T, D), F32),
        scratch_shapes=[pltpu.VMEM((2, MOE_TOP_K, tm * V7X_SUBLANES, V7X_LANES), F32),
                        pltpu.SemaphoreType.DMA((2,))],
        compiler_params=_cparams(1),
        name="moe_combine_ple",
    )(dest3, dest3, h2, ri, p3, y_rows, pn, wg, bg, wp, fnorm)


TM_PROJ = 256
TM_RES = 512
TM_TOKEN = 256
TM_MOVE = 256
EXPERT_ROWS = 512
ATTN_Q_BLOCK = 512
ATTN_K_BLOCK = 256
ATTN_HEADS = 4

def _row(v):
    return v.reshape(1, -1).astype(F32)


def _moe_ple(h2, p3, moe_norm, w_rg, b_rg, w_re, b_re, w_gate, w_up, w_down,
             ple_norm, ple_w_gate, ple_b_gate, ple_w_proj, final_norm, *, layer, last):
    T, D = h2.shape
    G = w_rg.shape[1]
    E = w_re.shape[1]
    TK = T * MOE_TOP_K
    pad = V7X_LANES - G - E
    w_r = jnp.concatenate([w_rg, w_re, jnp.zeros((D, pad), F32)], axis=1)
    b_r = jnp.concatenate([b_rg, b_re, jnp.zeros((pad,), F32)]).reshape(1, V7X_LANES)
    w_hi = w_r.astype(BF16)
    w_mid = (w_r - w_hi.astype(F32)).astype(BF16)
    ri, cnt = _router(h2, _row(moe_norm), jnp.concatenate([w_hi, w_mid], axis=1), b_r,
                      n_experts=E, tm=TM_TOKEN)

    rb = EXPERT_ROWS
    counts = cnt[0, G:G + E].astype(I32)
    padded = ((counts + rb - 1) // rb) * rb
    pad_end = jnp.cumsum(padded)
    pad_start = pad_end - padded
    n_blk = TK // rb + E
    n_rows = n_blk * rb
    eids = jnp.arange(E, dtype=I32)
    expert = ri[:, RI_E0:RI_E1 + 1].astype(I32)
    rank = ri[:, RI_R0:RI_R1 + 1].astype(I32)
    dest = jnp.sum(jnp.where(expert[:, :, None] == eids, pad_start, 0), axis=-1) + rank
    tmv = min(TM_MOVE, T)
    dest3 = dest.reshape(T // tmv, 1, MOE_TOP_K * tmv)
    blk_row0 = jnp.arange(n_blk, dtype=I32) * rb
    blk_expert = jnp.minimum(jnp.sum((pad_end[None, :] <= blk_row0[:, None]).astype(I32), axis=1), E - 1)
    n_used = (pad_end[-1:] // rb).astype(I32)

    x_rows = _dispatch(dest3, h2, _row(moe_norm), n_rows, tm=tmv)
    y_rows = _experts(blk_expert, n_used, x_rows, w_gate, w_up, w_down, layer=layer, rb=rb)
    return _combine(dest3, h2, ri, p3, y_rows, _row(ple_norm), ple_w_gate.astype(BF16), _row(ple_b_gate),
                    ple_w_proj.astype(BF16), _row(final_norm), layer=layer, tm=tmv, final_norm=last)


def _ssd_layer(h2, B, L, norm, w_in, conv_w, conv_b, dt_bias, a_log, d_skip, gnorm, w_out):
    T, D = h2.shape
    H = a_log.shape[0]
    d_inner = H * SSD_HEAD_DIM
    cdim = conv_w.shape[1]
    wz = w_in[:, :d_inner].astype(BF16)
    wx = w_in[:, d_inner:d_inner + cdim].astype(BF16)
    padh = V7X_LANES - H
    wdt = jnp.concatenate([w_in[:, d_inner + cdim:], jnp.zeros((D, padh), F32)], axis=1).astype(BF16)
    dtb = jnp.concatenate([dt_bias, jnp.zeros((padh,), F32)]).reshape(1, V7X_LANES)
    a_row = jnp.concatenate([-jnp.exp(a_log.astype(F32)), jnp.zeros((padh,), F32)]).reshape(1, V7X_LANES)
    dsk = jnp.repeat(d_skip.astype(F32), SSD_HEAD_DIM).reshape(1, d_inner)

    z, xbc, dt = _in_proj(h2, _row(norm), wz, wx, wdt, dtb, tm=TM_PROJ)
    yn = _ssd_scan(xbc.reshape(B, L, cdim), z.reshape(B, L, d_inner), dt.reshape(B, L, V7X_LANES),
                   conv_w.astype(F32), _row(conv_b), a_row, dsk, _row(gnorm), d_inner=d_inner)
    return _proj_res(yn.reshape(T, d_inner), w_out.astype(BF16), h2, tm=TM_RES)


def _sb_layer(h2, B, L, norm, w_qkv, w_o):
    T, D = h2.shape
    hd = SB_HEAD_DIM
    n_heads = w_o.shape[0] // hd
    q, k, v = _qkv_proj(h2.reshape(B, L, D), _row(norm), w_qkv.astype(BF16),
                        n_heads=n_heads, hd=hd, tm=TM_PROJ)
    bh = B * n_heads
    o = _attention(q.reshape(bh, L, hd), k.reshape(bh, L, hd), v.reshape(bh, L, hd),
                   B=B, n_heads=n_heads, tq=min(ATTN_Q_BLOCK, L), tk=ATTN_K_BLOCK, nh=ATTN_HEADS)
    return _proj_res(o.reshape(T, n_heads * hd), w_o.astype(BF16), h2, tm=TM_RES)


def kernel(x, p, ssd_norm, ssd_w_in, ssd_conv_w, ssd_conv_b, ssd_dt_bias, ssd_a_log, ssd_d, ssd_gnorm, ssd_w_out, sb_norm, sb_w_qkv, sb_w_o, moe_norm, moe_w_rg, moe_b_rg, moe_w_re, moe_b_re, moe_w_gate, moe_w_up, moe_w_down, ple_norm, ple_w_gate, ple_b_gate, ple_w_proj, final_norm):
    B, L, D = x.shape
    depth = p.shape[0]
    T = B * L
    n_mixers = 2
    h = x.reshape(T, D)
    p3 = p.reshape(depth, T, p.shape[-1])
    for i in range(depth):
        j = i // n_mixers
        if i % n_mixers == 0:
            h = _ssd_layer(h, B, L, ssd_norm[j], ssd_w_in[j], ssd_conv_w[j], ssd_conv_b[j], ssd_dt_bias[j],
                           ssd_a_log[j], ssd_d[j], ssd_gnorm[j], ssd_w_out[j])
        else:
            h = _sb_layer(h, B, L, sb_norm[j], sb_w_qkv[j], sb_w_o[j])
        h = _moe_ple(h, p3, moe_norm[i], moe_w_rg[i], moe_b_rg[i], moe_w_re[i], moe_b_re[i],
                     moe_w_gate, moe_w_up, moe_w_down, ple_norm[i], ple_w_gate[i], ple_b_gate[i],
                     ple_w_proj[i], final_norm, layer=i, last=(i == depth - 1))
    return h.reshape(B, L, D)
```

```python
import functools
import math

import jax
import jax.numpy as jnp
from jax import lax
from jax.experimental import pallas as pl
from jax.experimental.pallas import tpu as pltpu

F32 = jnp.float32
BF16 = jnp.bfloat16
I32 = jnp.int32

EPS = 1e-6
V7X_LANES = 128
V7X_SUBLANES = 8
V7X_VMEM_LIMIT_BYTES = 56 * 1024 * 1024

SSD_HEAD_DIM = 64
SSD_N_GROUPS = 4
SSD_D_STATE = 128
SSD_CHUNK = 128
SSD_CHUNKS_PER_STEP = 4
SSD_CONV = 4
SB_HEAD_DIM = 128
MOE_GROUPS = 8
MOE_TOP_K = 2

SCALAR_UNROLL = 8

ATTN_DEAD_LOG = -110.0


def _cparams(n_axes):
    return pltpu.CompilerParams(
        dimension_semantics=("arbitrary",) * n_axes,
        vmem_limit_bytes=V7X_VMEM_LIMIT_BYTES,
    )


def _dot(a, b):
    return jnp.dot(a, b, preferred_element_type=F32)


def _dot_nt(a, b):
    return lax.dot_general(a, b, (((1,), (1,)), ((), ())), preferred_element_type=F32)


def _split3(x):
    hi = x.astype(BF16)
    r1 = x - hi.astype(F32)
    mid = r1.astype(BF16)
    lo = (r1 - mid.astype(F32)).astype(BF16)
    return hi, mid, lo


def _split2(x):
    hi = x.astype(BF16)
    lo = (x - hi.astype(F32)).astype(BF16)
    return hi, lo


def _dot_exact_lhs(a_bf16, x):
    hi, mid, lo = _split3(x)
    return _dot(a_bf16, hi) + _dot(a_bf16, mid) + _dot(a_bf16, lo)


def _sigmoid(x):
    return 1.0 / (1.0 + jnp.exp(-x))


def _softplus(x):
    return jnp.maximum(x, 0.0) + jnp.log(1.0 + jnp.exp(-jnp.abs(x)))


def _rmsnorm(x, g):
    ms = jnp.mean(x * x, axis=-1, keepdims=True)
    return (x * lax.rsqrt(ms + EPS)) * g


def _tile_rows_load(ref, n):
    return jnp.concatenate(
        [ref[pl.ds(j, n, stride=V7X_SUBLANES), :] for j in range(V7X_SUBLANES)], axis=1)


def _tile_rows_store(ref, val, n):
    for j in range(V7X_SUBLANES):
        ref[pl.ds(j, n, stride=V7X_SUBLANES), :] = val[:, j * V7X_LANES:(j + 1) * V7X_LANES]


PACKED_SUBLANES = 4
_HIGH_HALF = 0xFFFF0000


def _packed_rows_store(ref, val, n):
    half = val.shape[1] // 2
    bits = lax.bitcast_convert_type(val.astype(BF16).astype(F32), jnp.uint32)
    words = bits[:, :half] | (bits[:, half:] >> 16)
    for j in range(PACKED_SUBLANES):
        ref[pl.ds(j, n, stride=PACKED_SUBLANES), :] = words[:, j * V7X_LANES:(j + 1) * V7X_LANES]


def _packed_rows_load(ref, n):
    words = jnp.concatenate(
        [ref[pl.ds(j, n, stride=PACKED_SUBLANES), :] for j in range(PACKED_SUBLANES)], axis=1)
    first = lax.bitcast_convert_type(words & jnp.uint32(_HIGH_HALF), F32)
    second = lax.bitcast_convert_type(words << 16, F32)
    return jnp.concatenate([first, second], axis=1).astype(BF16)


def _in_proj_kernel(x_ref, g_ref, wz_ref, wx_ref, wdt_ref, dtb_ref, z_ref, xbc_ref, dt_ref):
    xn = _rmsnorm(x_ref[...], g_ref[...]).astype(BF16)
    z_ref[...] = _dot(xn, wz_ref[...])
    xbc_ref[...] = _dot(xn, wx_ref[...])
    dt_ref[...] = _softplus(_dot(xn, wdt_ref[...]) + dtb_ref[...])


def _in_proj(x2, g, wz, wx, wdt, dtb, *, tm):
    T, D = x2.shape
    nz, nx, nd = wz.shape[1], wx.shape[1], wdt.shape[1]
    full = lambda i: (0, 0)
    return pl.pallas_call(
        _in_proj_kernel,
        grid=(T // tm,),
        in_specs=[
            pl.BlockSpec((tm, D), lambda i: (i, 0)),
            pl.BlockSpec((1, D), full),
            pl.BlockSpec((D, nz), full),
            pl.BlockSpec((D, nx), full),
            pl.BlockSpec((D, nd), full),
            pl.BlockSpec((1, nd), full),
        ],
        out_specs=[
            pl.BlockSpec((tm, nz), lambda i: (i, 0)),
            pl.BlockSpec((tm, nx), lambda i: (i, 0)),
            pl.BlockSpec((tm, nd), lambda i: (i, 0)),
        ],
        out_shape=[
            jax.ShapeDtypeStruct((T, nz), F32),
            jax.ShapeDtypeStruct((T, nx), F32),
            jax.ShapeDtypeStruct((T, nd), F32),
        ],
        compiler_params=_cparams(1),
        name="ssd_in_proj",
    )(x2, g, wz, wx, wdt, dtb)


def _ssd_kernel(xbc_ref, z_ref, dt_ref, cw_ref, cb_ref, a_ref, dsk_ref, gn_ref, e2_ref, yn_ref,
                state_ref, cbuf_ref, act_ref, yz_ref, exp_ref, *, Q, **cfg):
    @pl.when(pl.program_id(1) == 0)
    def _init():
        state_ref[...] = jnp.zeros_like(state_ref)
        cbuf_ref[0:V7X_SUBLANES, :] = jnp.zeros((V7X_SUBLANES, cbuf_ref.shape[1]), F32)

    for r0 in range(0, xbc_ref.shape[0], Q):
        rows = pl.ds(r0, Q)
        _ssd_chunk(xbc_ref.at[rows], z_ref.at[rows], dt_ref.at[rows], cw_ref, cb_ref, a_ref, dsk_ref, gn_ref,
                   e2_ref, yn_ref.at[rows], state_ref, cbuf_ref, act_ref, yz_ref, exp_ref, Q=Q, **cfg)


def _ssd_chunk(xbc_ref, z_ref, dt_ref, cw_ref, cb_ref, a_ref, dsk_ref, gn_ref, e2_ref, yn_ref,
               state_ref, cbuf_ref, act_ref, yz_ref, exp_ref, *, Q, d_inner, n_groups, d_state, head_dim):
    N = d_state
    gw = d_inner // n_groups
    pairs_per_group = gw // V7X_LANES
    halo = V7X_SUBLANES

    x_in = xbc_ref[...]
    cbuf_ref[halo:halo + Q, :] = x_in
    cw = cw_ref[...]
    conv = x_in * cw[SSD_CONV - 1:SSD_CONV, :] + cb_ref[...]
    for k in range(SSD_CONV - 1):
        off = halo - (SSD_CONV - 1) + k
        conv = conv + cbuf_ref[off:off + Q, :] * cw[k:k + 1, :]
    cbuf_ref[0:halo, :] = x_in[Q - halo:Q, :]
    act_ref[...] = conv * _sigmoid(conv)

    dt = dt_ref[...]
    rows = lax.broadcasted_iota(I32, (Q, Q), 0)
    cols = lax.broadcasted_iota(I32, (Q, Q), 1)
    causal = rows >= cols
    tril = jnp.where(causal, 1.0, 0.0).astype(BF16)
    acum = _dot_exact_lhs(tril, dt * a_ref[...])
    acum_t = acum.T
    a_last = acum[Q - 1:Q, :]
    e2 = e2_ref[...]
    for idx, v in enumerate((dt, jnp.exp(acum), jnp.exp(a_last - acum))):
        hi, lo = _split2(v)
        exp_ref[idx] = _dot(jnp.concatenate([hi, lo], axis=1), e2)
    DT, DIN, DTE = 0, 1, 2

    for g in range(n_groups):
        b_f = act_ref[:, d_inner + g * N:d_inner + (g + 1) * N]
        b_g = b_f.astype(BF16)
        c_g = act_ref[:, d_inner + n_groups * N + g * N:d_inner + n_groups * N + (g + 1) * N].astype(BF16)
        cb = _dot_nt(c_g, b_g)
        gsl = slice(g * gw, (g + 1) * gw)
        s_old = state_ref[:, gsl]
        y_off = _dot(c_g, s_old.astype(BF16)) * exp_ref[DIN, :, gsl]
        xdt_g = act_ref[:, gsl] * exp_ref[DT, :, gsl]
        for jj in range(pairs_per_group):
            j = g * pairs_per_group + jj
            h0 = 2 * j
            sl = slice(j * V7X_LANES, (j + 1) * V7X_LANES)
            lsl = slice(jj * V7X_LANES, (jj + 1) * V7X_LANES)
            xdt_b = xdt_g[:, lsl].astype(BF16)
            ys = []
            for hh in (h0, h0 + 1):
                seg = acum[:, hh:hh + 1] - acum_t[hh:hh + 1, :]
                dec = jnp.exp(jnp.where(causal, seg, -jnp.inf))
                ys.append(_dot((cb * dec).astype(BF16), xdt_b))
            lane = lax.broadcasted_iota(I32, (Q, V7X_LANES), 1)
            y = jnp.where(lane < head_dim, ys[0], ys[1])
            y = y + y_off[:, lsl] + act_ref[:, sl] * dsk_ref[:, sl]
            zp = z_ref[:, sl]
            yz_ref[:, sl] = y * (zp * _sigmoid(zp))
        xw_g = (xdt_g * exp_ref[DTE, :, gsl]).astype(BF16)
        new_states = _dot(b_f.T.astype(BF16), xw_g)
        state_ref[:, gsl] = s_old * exp_ref[DIN, Q - 1:Q, gsl] + new_states
        yzg = yz_ref[:, g * gw:(g + 1) * gw]
        ms = jnp.mean(yzg * yzg, axis=-1, keepdims=True)
        yn_ref[:, g * gw:(g + 1) * gw] = (
            (yzg * lax.rsqrt(ms + EPS)) * gn_ref[:, g * gw:(g + 1) * gw]).astype(yn_ref.dtype)


def _ssd_scan(xbc, z, dt, cw, cb, a_row, dsk, gn, *, d_inner):
    B, L, cdim = xbc.shape
    Q = SSD_CHUNK
    nd = dt.shape[-1]
    full2 = lambda b, c: (0, 0)
    head_of_col = jnp.arange(d_inner, dtype=I32) // SSD_HEAD_DIM
    e1 = (jnp.arange(nd, dtype=I32)[:, None] == head_of_col[None, :]).astype(BF16)
    e2 = jnp.concatenate([e1, e1], axis=0)
    kern = functools.partial(_ssd_kernel, Q=Q, d_inner=d_inner, n_groups=SSD_N_GROUPS,
                             d_state=SSD_D_STATE, head_dim=SSD_HEAD_DIM)
    rows = SSD_CHUNKS_PER_STEP * Q
    return pl.pallas_call(
        kern,
        grid=(B, L // rows),
        in_specs=[
            pl.BlockSpec((None, rows, cdim), lambda b, c: (b, c, 0)),
            pl.BlockSpec((None, rows, d_inner), lambda b, c: (b, c, 0)),
            pl.BlockSpec((None, rows, nd), lambda b, c: (b, c, 0)),
            pl.BlockSpec(cw.shape, full2),
            pl.BlockSpec(cb.shape, full2),
            pl.BlockSpec(a_row.shape, full2),
            pl.BlockSpec(dsk.shape, full2),
            pl.BlockSpec(gn.shape, full2),
            pl.BlockSpec(e2.shape, full2),
        ],
        out_specs=pl.BlockSpec((None, rows, d_inner), lambda b, c: (b, c, 0)),
        out_shape=jax.ShapeDtypeStruct((B, L, d_inner), BF16),
        scratch_shapes=[
            pltpu.VMEM((SSD_D_STATE, d_inner), F32),
            pltpu.VMEM((Q + V7X_SUBLANES, cdim), F32),
            pltpu.VMEM((Q, cdim), F32),
            pltpu.VMEM((Q, d_inner), F32),
            pltpu.VMEM((3, Q, d_inner), F32),
        ],
        compiler_params=_cparams(2),
        name="ssd_scan",
    )(xbc, z, dt, cw, cb, a_row, dsk, gn, e2)


def _proj_res_kernel(a_ref, w_ref, r_ref, o_ref):
    o_ref[...] = r_ref[...] + _dot(a_ref[...], w_ref[...])


def _proj_res(a, w, res, *, tm):
    T, K = a.shape
    N = w.shape[1]
    return pl.pallas_call(
        _proj_res_kernel,
        grid=(T // tm,),
        in_specs=[
            pl.BlockSpec((tm, K), lambda i: (i, 0)),
            pl.BlockSpec((K, N), lambda i: (0, 0)),
            pl.BlockSpec((tm, N), lambda i: (i, 0)),
        ],
        out_specs=pl.BlockSpec((tm, N), lambda i: (i, 0)),
        out_shape=jax.ShapeDtypeStruct((T, N), F32),
        compiler_params=_cparams(1),
        name="proj_residual",
    )(a, w, res)


def _qkv_kernel(x_ref, g_ref, w_ref, q_ref, k_ref, v_ref, *, n_heads, hd, scale):
    xn = _rmsnorm(x_ref[...], g_ref[...]).astype(BF16)
    qkv = _dot(xn, w_ref[...])
    d = n_heads * hd
    for h in range(n_heads):
        q_ref[h] = (qkv[:, h * hd:(h + 1) * hd] * scale).astype(BF16)
        k_ref[h] = qkv[:, d + h * hd:d + (h + 1) * hd].astype(BF16)
        v_ref[h] = qkv[:, 2 * d + h * hd:2 * d + (h + 1) * hd].astype(BF16)


def _qkv_proj(x, g, w, *, n_heads, hd, tm):
    B, L, D = x.shape
    kern = functools.partial(_qkv_kernel, n_heads=n_heads, hd=hd, scale=1.0 / math.sqrt(hd))
    hspec = pl.BlockSpec((None, n_heads, tm, hd), lambda b, i: (b, 0, i, 0))
    hshape = jax.ShapeDtypeStruct((B, n_heads, L, hd), BF16)
    return pl.pallas_call(
        kern,
        grid=(B, L // tm),
        in_specs=[
            pl.BlockSpec((None, tm, D), lambda b, i: (b, i, 0)),
            pl.BlockSpec((1, D), lambda b, i: (0, 0)),
            pl.BlockSpec(w.shape, lambda b, i: (0, 0)),
        ],
        out_specs=[hspec, hspec, hspec],
        out_shape=[hshape, hshape, hshape],
        compiler_params=_cparams(2),
        name="sb_qkv_proj",
    )(x, g, w)


def _attn_kernel(q_ref, k_ref, v_ref, o_ref, acc_ref, carry_ref, *, tq, tk):
    qi = pl.program_id(1)
    nh, _, hd = q_ref.shape
    n_diag = tq // tk
    rows_t = lax.broadcasted_iota(I32, (tk, tk), 0)
    cols_t = lax.broadcasted_iota(I32, (tk, tk), 1)
    tri = jnp.where(rows_t >= cols_t, 1.0, 0.0).astype(BF16)
    tri2 = jnp.concatenate([tri, tri], axis=0)

    def block(k0, r0, diag):
        nr = tq - r0
        if diag:
            before = (lax.broadcasted_iota(I32, (nr, tk), 1) < lax.broadcasted_iota(I32, (nr, tk), 0))
        zs, sps = [], []
        for h in range(nh):
            z = _dot_nt(q_ref[h, r0:tq, :], k_ref[h, pl.ds(k0, tk), :])
            sp = _softplus(z)
            if diag:
                sp = jnp.where(before, sp, 0.0)
            zs.append(z)
            sps.append(sp)
        hi, lo = _split2(jnp.concatenate(sps, axis=0))
        cs_all = _dot(jnp.concatenate([hi, lo], axis=1), tri2)
        for h in range(nh):
            cs = cs_all[h * nr:(h + 1) * nr, :]
            carry = carry_ref[h, r0:tq, 0:1]
            log_a = zs[h] - cs + carry
            if diag:
                log_a = jnp.where(before, log_a, -jnp.inf)
            a = jnp.exp(log_a)
            acc_ref[h, r0:tq, :] += _dot(a.astype(BF16), v_ref[h, pl.ds(k0, tk), :])
            carry_ref[h, r0:tq, :] = jnp.broadcast_to(carry - cs[:, 0:1], (nr, carry_ref.shape[2]))

    acc_ref[...] = jnp.zeros_like(acc_ref)
    carry_ref[...] = jnp.zeros_like(carry_ref)
    q0 = qi * tq
    for d in range(n_diag - 1, -1, -1):
        block(pl.multiple_of(q0 + d * tk, tk), d * tk, True)

    def cond(st):
        kb, live = st
        return jnp.logical_and(kb >= 0, live > ATTN_DEAD_LOG)

    def body(st):
        kb, _ = st
        block(pl.multiple_of(kb * tk, tk), 0, False)
        return kb - 1, jnp.max(carry_ref[:, :, 0:1])

    lax.while_loop(cond, body, (qi * n_diag - 1, jnp.max(carry_ref[:, :, 0:1])))
    for h in range(nh):
        o_ref[:, h * hd:(h + 1) * hd] = acc_ref[h].astype(o_ref.dtype)


def _attention(q, k, v, *, B, n_heads, tq, tk, nh):
    BH, L, hd = q.shape
    kern = functools.partial(_attn_kernel, tq=tq, tk=tk)
    groups = n_heads // nh
    return pl.pallas_call(
        kern,
        grid=(BH // nh, L // tq),
        in_specs=[
            pl.BlockSpec((nh, tq, hd), lambda g, i: (g, i, 0)),
            pl.BlockSpec((nh, L, hd), lambda g, i: (g, 0, 0), pipeline_mode=pl.Buffered(1)),
            pl.BlockSpec((nh, L, hd), lambda g, i: (g, 0, 0), pipeline_mode=pl.Buffered(1)),
        ],
        out_specs=pl.BlockSpec((None, tq, nh * hd), lambda g, i: (g // groups, i, g % groups)),
        out_shape=jax.ShapeDtypeStruct((B, L, n_heads * hd), BF16),
        scratch_shapes=[pltpu.VMEM((nh, tq, hd), F32), pltpu.VMEM((nh, tq, V7X_LANES), F32)],
        compiler_params=_cparams(2),
        name="sb_attention",
    )(q, k, v)


RI_E0, RI_E1, RI_G0, RI_G1, RI_R0, RI_R1 = 0, 1, 2, 3, 4, 5


def _router_kernel(h_ref, g_ref, whm_ref, b_ref, ri_ref, cnt_ref, carry_ref, *, n_groups, n_experts):
    i = pl.program_id(0)
    tm = h_ref.shape[0]
    epg = n_experts // n_groups

    @pl.when(i == 0)
    def _init():
        carry_ref[...] = jnp.zeros_like(carry_ref)

    xn = _rmsnorm(h_ref[...], g_ref[...])
    xh, xm = _split2(xn)
    both = _dot(xh, whm_ref[...])
    logits = (both[:, :V7X_LANES] + (both[:, V7X_LANES:] + _dot(xm, whm_ref[:, :V7X_LANES]))) + b_ref[...]

    lane = lax.broadcasted_iota(I32, (tm, V7X_LANES), 1).astype(F32)
    neg = -jnp.inf
    no_lane = float(V7X_LANES)
    gl = jnp.where(lane < n_groups, logits, neg)
    gmax = jnp.max(gl, axis=-1, keepdims=True)
    gidx = jnp.min(jnp.where(gl == gmax, lane, no_lane), axis=-1, keepdims=True)
    g_w = 1.0 / jnp.sum(jnp.exp(gl - gmax), axis=-1, keepdims=True)

    first = n_groups + gidx * epg
    in_group = jnp.logical_and(lane >= first, lane < first + epg)
    el = jnp.where(in_group, logits, neg)
    m0 = jnp.max(el, axis=-1, keepdims=True)
    i0 = jnp.min(jnp.where(el == m0, lane, no_lane), axis=-1, keepdims=True)
    el1 = jnp.where(lane == i0, neg, el)
    m1 = jnp.max(el1, axis=-1, keepdims=True)
    i1 = jnp.min(jnp.where(el1 == m1, lane, no_lane), axis=-1, keepdims=True)
    d = jnp.exp(m1 - m0)
    p0 = 1.0 / (1.0 + d)
    gate0 = g_w * p0
    gate1 = g_w * (d * p0)

    oh0 = jnp.where(lane == i0, 1.0, 0.0)
    oh1 = jnp.where(lane == i1, 1.0, 0.0)
    r = lax.broadcasted_iota(I32, (tm, tm), 0)
    cc = lax.broadcasted_iota(I32, (tm, tm), 1)
    strict = jnp.where(cc < r, 1.0, 0.0).astype(BF16)
    pre0 = _dot(strict, oh0.astype(BF16))
    pre1 = _dot(strict, oh1.astype(BF16))
    cnt0 = jnp.sum(oh0, axis=0, keepdims=True)
    cnt1 = jnp.sum(oh1, axis=0, keepdims=True)
    base = carry_ref[0:1, :]
    rank0 = jnp.sum(oh0 * (pre0 + base), axis=-1, keepdims=True)
    rank1 = jnp.sum(oh1 * (pre1 + (base + cnt0)), axis=-1, keepdims=True)
    total = base + cnt0 + cnt1
    carry_ref[...] = jnp.broadcast_to(total, carry_ref.shape)
    cnt_ref[...] = jnp.broadcast_to(total, cnt_ref.shape)

    e0 = i0 - n_groups
    e1 = i1 - n_groups
    ri = jnp.where(lane == RI_E0, e0, 0.0)
    ri = jnp.where(lane == RI_E1, e1, ri)
    ri = jnp.where(lane == RI_G0, gate0, ri)
    ri = jnp.where(lane == RI_G1, gate1, ri)
    ri = jnp.where(lane == RI_R0, rank0, ri)
    ri = jnp.where(lane == RI_R1, rank1, ri)
    ri_ref[...] = ri


def _router(h2, g, w_hm, b_r, *, n_experts, tm):
    T, D = h2.shape
    kern = functools.partial(_router_kernel, n_groups=MOE_GROUPS, n_experts=n_experts)
    return pl.pallas_call(
        kern,
        grid=(T // tm,),
        in_specs=[
            pl.BlockSpec((tm, D), lambda i: (i, 0)),
            pl.BlockSpec((1, D), lambda i: (0, 0)),
            pl.BlockSpec((D, 2 * V7X_LANES), lambda i: (0, 0)),
            pl.BlockSpec((1, V7X_LANES), lambda i: (0, 0)),
        ],
        out_specs=[
            pl.BlockSpec((tm, V7X_LANES), lambda i: (i, 0)),
            pl.BlockSpec((V7X_SUBLANES, V7X_LANES), lambda i: (0, 0)),
        ],
        out_shape=[
            jax.ShapeDtypeStruct((T, V7X_LANES), F32),
            jax.ShapeDtypeStruct((V7X_SUBLANES, V7X_LANES), F32),
        ],
        scratch_shapes=[pltpu.VMEM((V7X_SUBLANES, V7X_LANES), F32)],
        compiler_params=_cparams(1),
        name="moe_router",
    )(h2, g, w_hm, b_r)


def _tile_copy(src_ref, src_row, dst_ref, dst_row, sem, sub=V7X_SUBLANES):
    s = pl.multiple_of(src_row * sub, sub)
    d = pl.multiple_of(dst_row * sub, sub)
    return pltpu.make_async_copy(src_ref.at[pl.ds(s, sub)], dst_ref.at[pl.ds(d, sub)], sem)


def _token_loop(tm, fn):
    def step(t, c):
        fn(t)
        return c

    lax.fori_loop(0, tm, step, 0, unroll=SCALAR_UNROLL)


def _dispatch_kernel(dest_ref, h_ref, g_ref, init_ref, rows_ref, xt_ref, sem):
    del init_ref
    i = pl.program_id(0)
    tm = h_ref.shape[0]
    cur = i % 2
    _packed_rows_store(xt_ref.at[cur], _rmsnorm(h_ref[...], g_ref[...]), tm)

    def issue(t):
        for k in range(MOE_TOP_K):
            _tile_copy(xt_ref.at[cur], t, rows_ref, dest_ref[0, 0, MOE_TOP_K * t + k],
                       sem.at[cur], PACKED_SUBLANES).start(priority=k)

    def drain(b):
        def wait(t):
            for _ in range(MOE_TOP_K):
                _tile_copy(xt_ref.at[b], t, rows_ref, 0, sem.at[b], PACKED_SUBLANES).wait()

        _token_loop(tm, wait)

    _token_loop(tm, issue)

    @pl.when(i >= 1)
    def _prev():
        drain(1 - cur)

    @pl.when(i == pl.num_programs(0) - 1)
    def _last():
        drain(cur)


def _dispatch(dest3, h2, g, n_rows, *, tm):
    T, D = h2.shape
    assert D == 2 * PACKED_SUBLANES * V7X_LANES
    tiles = (n_rows * PACKED_SUBLANES, V7X_LANES)
    init = jnp.zeros(tiles, jnp.uint32)
    return pl.pallas_call(
        _dispatch_kernel,
        grid=(T // tm,),
        in_specs=[
            pl.BlockSpec((1, 1, MOE_TOP_K * tm), lambda i: (i, 0, 0), memory_space=pltpu.SMEM),
            pl.BlockSpec((tm, D), lambda i: (i, 0)),
            pl.BlockSpec((1, D), lambda i: (0, 0)),
            pl.BlockSpec(memory_space=pl.ANY),
        ],
        out_specs=pl.BlockSpec(memory_space=pl.ANY),
        out_shape=jax.ShapeDtypeStruct(tiles, jnp.uint32),
        scratch_shapes=[pltpu.VMEM((2, tm * PACKED_SUBLANES, V7X_LANES), jnp.uint32),
                        pltpu.SemaphoreType.DMA((2,))],
        input_output_aliases={3: 0},
        compiler_params=_cparams(1),
        name="moe_dispatch",
    )(dest3, h2, g, init)


def _expert_kernel(be_ref, nu_ref, x_ref, wg_ref, wu_ref, wd_ref, y_ref):
    i = pl.program_id(0)
    rb = x_ref.shape[0] // PACKED_SUBLANES

    @pl.when(i < nu_ref[0])
    def _live():
        x = _packed_rows_load(x_ref, rb)
        gt = _dot(x, wg_ref[0].astype(BF16))
        up = _dot(x, wu_ref[0].astype(BF16))
        act = (gt * _sigmoid(gt)) * up
        _tile_rows_store(y_ref, _dot(act.astype(BF16), wd_ref[0].astype(BF16)), rb)

    @pl.when(i >= nu_ref[0])
    def _dead():
        y_ref[...] = jnp.zeros_like(y_ref)


def _experts(blk_expert, n_used, x_rows, w_gate, w_up, w_down, *, layer, rb):
    n_blk = x_rows.shape[0] // (rb * PACKED_SUBLANES)
    _, E, D, F = w_gate.shape
    in_blk = (rb * PACKED_SUBLANES, V7X_LANES)
    tile_blk = (rb * V7X_SUBLANES, V7X_LANES)

    def row_map(i, be, nu):
        return (jnp.minimum(i, nu[0] - 1), 0)

    def w_map(i, be, nu):
        return (layer, be[i], 0, 0)

    grid_spec = pltpu.PrefetchScalarGridSpec(
        num_scalar_prefetch=2,
        grid=(n_blk,),
        in_specs=[
            pl.BlockSpec(in_blk, row_map),
            pl.BlockSpec((None, 1, D, F), w_map),
            pl.BlockSpec((None, 1, D, F), w_map),
            pl.BlockSpec((None, 1, F, D), w_map),
        ],
        out_specs=pl.BlockSpec(tile_blk, lambda i, be, nu: (i, 0)),
    )
    return pl.pallas_call(
        _expert_kernel,
        grid_spec=grid_spec,
        out_shape=jax.ShapeDtypeStruct((n_blk * rb * V7X_SUBLANES, V7X_LANES), F32),
        compiler_params=_cparams(1),
        name="moe_experts",
    )(blk_expert, n_used, x_rows, w_gate, w_up, w_down)


def _combine_kernel(dest_ref, destn_ref, h_ref, ri_ref, p_ref, yrows_ref, pn_ref, wg_ref, bg_ref, wp_ref,
                    fn_ref, o_ref, ybuf_ref, sem, *, final_norm):
    i = pl.program_id(0)
    tm = h_ref.shape[0]
    cur = i % 2

    def fetch(d_ref, b):
        def issue(t):
            for k in range(MOE_TOP_K):
                _tile_copy(yrows_ref, d_ref[0, 0, MOE_TOP_K * t + k], ybuf_ref.at[b, k], t,
                           sem.at[b]).start(priority=k)

        _token_loop(tm, issue)

    @pl.when(i == 0)
    def _first():
        fetch(dest_ref, 0)

    @pl.when(i + 1 < pl.num_programs(0))
    def _next():
        fetch(destn_ref, 1 - cur)

    def wait(t):
        for k in range(MOE_TOP_K):
            _tile_copy(yrows_ref, 0, ybuf_ref.at[cur, k], t, sem.at[cur]).wait()

    _token_loop(tm, wait)

    ri = ri_ref[...]
    h = h_ref[...] + (_tile_rows_load(ybuf_ref.at[cur, 0], tm) * ri[:, RI_G0:RI_G0 + 1]
                      + _tile_rows_load(ybuf_ref.at[cur, 1], tm) * ri[:, RI_G1:RI_G1 + 1])
    xn = _rmsnorm(h, pn_ref[...]).astype(BF16)
    gate = _sigmoid(_dot(xn, wg_ref[...]) + bg_ref[...])
    h = h + gate * _dot(p_ref[...].astype(BF16), wp_ref[...])
    if final_norm:
        h = _rmsnorm(h, fn_ref[...])
    o_ref[...] = h


def _combine(dest3, h2, ri, p3, y_rows, pn, wg, bg, wp, fnorm, *, layer, tm, final_norm):
    T, D = h2.shape
    P = p3.shape[2]
    kern = functools.partial(_combine_kernel, final_norm=final_norm)
    row = lambda i: (0, 0)
    last = T // tm - 1
    return pl.pallas_call(
        kern,
        grid=(T // tm,),
        in_specs=[
            pl.BlockSpec((1, 1, MOE_TOP_K * tm), lambda i: (i, 0, 0), memory_space=pltpu.SMEM),
            pl.BlockSpec((1, 1, MOE_TOP_K * tm), lambda i: (jnp.minimum(i + 1, last), 0, 0),
                         memory_space=pltpu.SMEM),
            pl.BlockSpec((tm, D), lambda i: (i, 0)),
            pl.BlockSpec((tm, V7X_LANES), lambda i: (i, 0)),
            pl.BlockSpec((None, tm, P), lambda i: (layer, i, 0)),
            pl.BlockSpec(memory_space=pl.ANY),
            pl.BlockSpec((1, D), row),
            pl.BlockSpec((D, D), row),
            pl.BlockSpec((1, D), row),
            pl.BlockSpec((P, D), row),
            pl.BlockSpec((1, D), row),
        ],
        out_specs=pl.BlockSpec((tm, D), lambda i: (i, 0)),
        out_shape=jax.ShapeDtypeStruct((T, D), F32),
        scratch_shapes=[pltpu.VMEM((2, MOE_TOP_K, tm * V7X_SUBLANES, V7X_LANES), F32),
                        pltpu.SemaphoreType.DMA((2,))],
        compiler_params=_cparams(1),
        name="moe_combine_ple",
    )(dest3, dest3, h2, ri, p3, y_rows, pn, wg, bg, wp, fnorm)


TM_PROJ = 256
TM_RES = 512
TM_TOKEN = 256
TM_MOVE = 256
EXPERT_ROWS = 512
ATTN_Q_BLOCK = 512
ATTN_K_BLOCK = 256
ATTN_HEADS = 4

def _row(v):
    return v.reshape(1, -1).astype(F32)


def _moe_ple(h2, p3, moe_norm, w_rg, b_rg, w_re, b_re, w_gate, w_up, w_down,
             ple_norm, ple_w_gate, ple_b_gate, ple_w_proj, final_norm, *, layer, last):
    T, D = h2.shape
    G = w_rg.shape[1]
    E = w_re.shape[1]
    TK = T * MOE_TOP_K
    pad = V7X_LANES - G - E
    w_r = jnp.concatenate([w_rg, w_re, jnp.zeros((D, pad), F32)], axis=1)
    b_r = jnp.concatenate([b_rg, b_re, jnp.zeros((pad,), F32)]).reshape(1, V7X_LANES)
    w_hi = w_r.astype(BF16)
    w_mid = (w_r - w_hi.astype(F32)).astype(BF16)
    ri, cnt = _router(h2, _row(moe_norm), jnp.concatenate([w_hi, w_mid], axis=1), b_r,
                      n_experts=E, tm=TM_TOKEN)

    rb = EXPERT_ROWS
    counts = cnt[0, G:G + E].astype(I32)
    padded = ((counts + rb - 1) // rb) * rb
    pad_end = jnp.cumsum(padded)
    pad_start = pad_end - padded
    n_blk = TK // rb + E
    n_rows = n_blk * rb
    eids = jnp.arange(E, dtype=I32)
    expert = ri[:, RI_E0:RI_E1 + 1].astype(I32)
    rank = ri[:, RI_R0:RI_R1 + 1].astype(I32)
    dest = jnp.sum(jnp.where(expert[:, :, None] == eids, pad_start, 0), axis=-1) + rank
    tmv = min(TM_MOVE, T)
    dest3 = dest.reshape(T // tmv, 1, MOE_TOP_K * tmv)
    blk_row0 = jnp.arange(n_blk, dtype=I32) * rb
    blk_expert = jnp.minimum(jnp.sum((pad_end[None, :] <= blk_row0[:, None]).astype(I32), axis=1), E - 1)
    n_used = (pad_end[-1:] // rb).astype(I32)

    x_rows = _dispatch(dest3, h2, _row(moe_norm), n_rows, tm=tmv)
    y_rows = _experts(blk_expert, n_used, x_rows, w_gate, w_up, w_down, layer=layer, rb=rb)
    return _combine(dest3, h2, ri, p3, y_rows, _row(ple_norm), ple_w_gate.astype(BF16), _row(ple_b_gate),
                    ple_w_proj.astype(BF16), _row(final_norm), layer=layer, tm=tmv, final_norm=last)


def _ssd_layer(h2, B, L, norm, w_in, conv_w, conv_b, dt_bias, a_log, d_skip, gnorm, w_out):
    T, D = h2.shape
    H = a_log.shape[0]
    d_inner = H * SSD_HEAD_DIM
    cdim = conv_w.shape[1]
    wz = w_in[:, :d_inner].astype(BF16)
    wx = w_in[:, d_inner:d_inner + cdim].astype(BF16)
    padh = V7X_LANES - H
    wdt = jnp.concatenate([w_in[:, d_inner + cdim:], jnp.zeros((D, padh), F32)], axis=1).astype(BF16)
    dtb = jnp.concatenate([dt_bias, jnp.zeros((padh,), F32)]).reshape(1, V7X_LANES)
    a_row = jnp.concatenate([-jnp.exp(a_log.astype(F32)), jnp.zeros((padh,), F32)]).reshape(1, V7X_LANES)
    dsk = jnp.repeat(d_skip.astype(F32), SSD_HEAD_DIM).reshape(1, d_inner)

    z, xbc, dt = _in_proj(h2, _row(norm), wz, wx, wdt, dtb, tm=TM_PROJ)
    yn = _ssd_scan(xbc.reshape(B, L, cdim), z.reshape(B, L, d_inner), dt.reshape(B, L, V7X_LANES),
                   conv_w.astype(F32), _row(conv_b), a_row, dsk, _row(gnorm), d_inner=d_inner)
    return _proj_res(yn.reshape(T, d_inner), w_out.astype(BF16), h2, tm=TM_RES)


def _sb_layer(h2, B, L, norm, w_qkv, w_o):
    T, D = h2.shape
    hd = SB_HEAD_DIM
    n_heads = w_o.shape[0] // hd
    q, k, v = _qkv_proj(h2.reshape(B, L, D), _row(norm), w_qkv.astype(BF16),
                        n_heads=n_heads, hd=hd, tm=TM_PROJ)
    bh = B * n_heads
    o = _attention(q.reshape(bh, L, hd), k.reshape(bh, L, hd), v.reshape(bh, L, hd),
                   B=B, n_heads=n_heads, tq=min(ATTN_Q_BLOCK, L), tk=ATTN_K_BLOCK, nh=ATTN_HEADS)
    return _proj_res(o.reshape(T, n_heads * hd), w_o.astype(BF16), h2, tm=TM_RES)


def kernel(x, p, ssd_norm, ssd_w_in, ssd_conv_w, ssd_conv_b, ssd_dt_bias, ssd_a_log, ssd_d, ssd_gnorm, ssd_w_out, sb_norm, sb_w_qkv, sb_w_o, moe_norm, moe_w_rg, moe_b_rg, moe_w_re, moe_b_re, moe_w_gate, moe_w_up, moe_w_down, ple_norm, ple_w_gate, ple_b_gate, ple_w_proj, final_norm):
    B, L, D = x.shape
    depth = p.shape[0]
    T = B * L
    n_mixers = 2
    h = x.reshape(T, D)
    p3 = p.reshape(depth, T, p.shape[-1])
    for i in range(depth):
        j = i // n_mixers
        if i % n_mixers == 0:
            h = _ssd_layer(h, B, L, ssd_norm[j], ssd_w_in[j], ssd_conv_w[j], ssd_conv_b[j], ssd_dt_bias[j],
                           ssd_a_log[j], ssd_d[j], ssd_gnorm[j], ssd_w_out[j])
        else:
            h = _sb_layer(h, B, L, sb_norm[j], sb_w_qkv[j], sb_w_o[j])
        h = _moe_ple(h, p3, moe_norm[i], moe_w_rg[i], moe_b_rg[i], moe_w_re[i], moe_b_re[i],
                     moe_w_gate, moe_w_up, moe_w_down, ple_norm[i], ple_w_gate[i], ple_b_gate[i],
                     ple_w_proj[i], final_norm, layer=i, last=(i == depth - 1))
    return h.reshape(B, L, D)
```

```python
import functools
import math

import jax
import jax.numpy as jnp
from jax import lax
from jax.experimental import pallas as pl
from jax.experimental.pallas import tpu as pltpu

F32 = jnp.float32
BF16 = jnp.bfloat16
I32 = jnp.int32

EPS = 1e-6
V7X_LANES = 128
V7X_SUBLANES = 8
V7X_VMEM_LIMIT_BYTES = 56 * 1024 * 1024

SSD_HEAD_DIM = 64
SSD_N_GROUPS = 4
SSD_D_STATE = 128
SSD_CHUNK = 128
SSD_CHUNKS_PER_STEP = 4
SSD_CONV = 4
SB_HEAD_DIM = 128
MOE_GROUPS = 8
MOE_TOP_K = 2

SCALAR_UNROLL = 8

ATTN_DEAD_LOG = -110.0


def _cparams(n_axes):
    return pltpu.CompilerParams(
        dimension_semantics=("arbitrary",) * n_axes,
        vmem_limit_bytes=V7X_VMEM_LIMIT_BYTES,
    )


def _dot(a, b):
    return jnp.dot(a, b, preferred_element_type=F32)


def _dot_nt(a, b):
    return lax.dot_general(a, b, (((1,), (1,)), ((), ())), preferred_element_type=F32)


def _split3(x):
    hi = x.astype(BF16)
    r1 = x - hi.astype(F32)
    mid = r1.astype(BF16)
    lo = (r1 - mid.astype(F32)).astype(BF16)
    return hi, mid, lo


def _split2(x):
    hi = x.astype(BF16)
    lo = (x - hi.astype(F32)).astype(BF16)
    return hi, lo


def _dot_exact_lhs(a_bf16, x):
    hi, mid, lo = _split3(x)
    return _dot(a_bf16, hi) + _dot(a_bf16, mid) + _dot(a_bf16, lo)


def _sigmoid(x):
    return 1.0 / (1.0 + jnp.exp(-x))


def _softplus(x):
    return jnp.maximum(x, 0.0) + jnp.log(1.0 + jnp.exp(-jnp.abs(x)))


def _rmsnorm(x, g):
    ms = jnp.mean(x * x, axis=-1, keepdims=True)
    return (x * lax.rsqrt(ms + EPS)) * g


def _tile_rows_load(ref, n):
    return jnp.concatenate(
        [ref[pl.ds(j, n, stride=V7X_SUBLANES), :] for j in range(V7X_SUBLANES)], axis=1)


def _tile_rows_store(ref, val, n):
    for j in range(V7X_SUBLANES):
        ref[pl.ds(j, n, stride=V7X_SUBLANES), :] = val[:, j * V7X_LANES:(j + 1) * V7X_LANES]


PACKED_SUBLANES = 4
_HIGH_HALF = 0xFFFF0000


def _packed_rows_store(ref, val, n):
    half = val.shape[1] // 2
    bits = lax.bitcast_convert_type(val.astype(BF16).astype(F32), jnp.uint32)
    words = bits[:, :half] | (bits[:, half:] >> 16)
    for j in range(PACKED_SUBLANES):
        ref[pl.ds(j, n, stride=PACKED_SUBLANES), :] = words[:, j * V7X_LANES:(j + 1) * V7X_LANES]


def _packed_rows_load(ref, n):
    words = jnp.concatenate(
        [ref[pl.ds(j, n, stride=PACKED_SUBLANES), :] for j in range(PACKED_SUBLANES)], axis=1)
    first = lax.bitcast_convert_type(words & jnp.uint32(_HIGH_HALF), F32)
    second = lax.bitcast_convert_type(words << 16, F32)
    return jnp.concatenate([first, second], axis=1).astype(BF16)


def _in_proj_kernel(x_ref, g_ref, wz_ref, wx_ref, wdt_ref, dtb_ref, z_ref, xbc_ref, dt_ref):
    xn = _rmsnorm(x_ref[...], g_ref[...]).astype(BF16)
    z_ref[...] = _dot(xn, wz_ref[...])
    xbc_ref[...] = _dot(xn, wx_ref[...])
    dt_ref[...] = _softplus(_dot(xn, wdt_ref[...]) + dtb_ref[...])


def _in_proj(x2, g, wz, wx, wdt, dtb, *, tm):
    T, D = x2.shape
    nz, nx, nd = wz.shape[1], wx.shape[1], wdt.shape[1]
    full = lambda i: (0, 0)
    return pl.pallas_call(
        _in_proj_kernel,
        grid=(T // tm,),
        in_specs=[
            pl.BlockSpec((tm, D), lambda i: (i, 0)),
            pl.BlockSpec((1, D), full),
            pl.BlockSpec((D, nz), full),
            pl.BlockSpec((D, nx), full),
            pl.BlockSpec((D, nd), full),
            pl.BlockSpec((1, nd), full),
        ],
        out_specs=[
            pl.BlockSpec((tm, nz), lambda i: (i, 0)),
            pl.BlockSpec((tm, nx), lambda i: (i, 0)),
            pl.BlockSpec((tm, nd), lambda i: (i, 0)),
        ],
        out_shape=[
            jax.ShapeDtypeStruct((T, nz), F32),
            jax.ShapeDtypeStruct((T, nx), F32),
            jax.ShapeDtypeStruct((T, nd), F32),
        ],
        compiler_params=_cparams(1),
        name="ssd_in_proj",
    )(x2, g, wz, wx, wdt, dtb)


def _ssd_kernel(xbc_ref, z_ref, dt_ref, cw_ref, cb_ref, a_ref, dsk_ref, gn_ref, e2_ref, yn_ref,
                state_ref, cbuf_ref, act_ref, yz_ref, exp_ref, *, Q, **cfg):
    @pl.when(pl.program_id(1) == 0)
    def _init():
        state_ref[...] = jnp.zeros_like(state_ref)
        cbuf_ref[0:V7X_SUBLANES, :] = jnp.zeros((V7X_SUBLANES, cbuf_ref.shape[1]), F32)

    for r0 in range(0, xbc_ref.shape[0], Q):
        rows = pl.ds(r0, Q)
        _ssd_chunk(xbc_ref.at[rows], z_ref.at[rows], dt_ref.at[rows], cw_ref, cb_ref, a_ref, dsk_ref, gn_ref,
                   e2_ref, yn_ref.at[rows], state_ref, cbuf_ref, act_ref, yz_ref, exp_ref, Q=Q, **cfg)


def _ssd_chunk(xbc_ref, z_ref, dt_ref, cw_ref, cb_ref, a_ref, dsk_ref, gn_ref, e2_ref, yn_ref,
               state_ref, cbuf_ref, act_ref, yz_ref, exp_ref, *, Q, d_inner, n_groups, d_state, head_dim):
    N = d_state
    gw = d_inner // n_groups
    pairs_per_group = gw // V7X_LANES
    halo = V7X_SUBLANES

    x_in = xbc_ref[...]
    cbuf_ref[halo:halo + Q, :] = x_in
    cw = cw_ref[...]
    conv = x_in * cw[SSD_CONV - 1:SSD_CONV, :] + cb_ref[...]
    for k in range(SSD_CONV - 1):
        off = halo - (SSD_CONV - 1) + k
        conv = conv + cbuf_ref[off:off + Q, :] * cw[k:k + 1, :]
    cbuf_ref[0:halo, :] = x_in[Q - halo:Q, :]
    act_ref[...] = conv * _sigmoid(conv)

    dt = dt_ref[...]
    rows = lax.broadcasted_iota(I32, (Q, Q), 0)
    cols = lax.broadcasted_iota(I32, (Q, Q), 1)
    causal = rows >= cols
    tril = jnp.where(causal, 1.0, 0.0).astype(BF16)
    acum = _dot_exact_lhs(tril, dt * a_ref[...])
    acum_t = acum.T
    a_last = acum[Q - 1:Q, :]
    e2 = e2_ref[...]
    for idx, v in enumerate((dt, jnp.exp(acum), jnp.exp(a_last - acum))):
        hi, lo = _split2(v)
        exp_ref[idx] = _dot(jnp.concatenate([hi, lo], axis=1), e2)
    DT, DIN, DTE = 0, 1, 2

    for g in range(n_groups):
        b_f = act_ref[:, d_inner + g * N:d_inner + (g + 1) * N]
        b_g = b_f.astype(BF16)
        c_g = act_ref[:, d_inner + n_groups * N + g * N:d_inner + n_groups * N + (g + 1) * N].astype(BF16)
        cb = _dot_nt(c_g, b_g)
        gsl = slice(g * gw, (g + 1) * gw)
        s_old = state_ref[:, gsl]
        y_off = _dot(c_g, s_old.astype(BF16)) * exp_ref[DIN, :, gsl]
        xdt_g = act_ref[:, gsl] * exp_ref[DT, :, gsl]
        for jj in range(pairs_per_group):
            j = g * pairs_per_group + jj
            h0 = 2 * j
            sl = slice(j * V7X_LANES, (j + 1) * V7X_LANES)
            lsl = slice(jj * V7X_LANES, (jj + 1) * V7X_LANES)
            xdt_b = xdt_g[:, lsl].astype(BF16)
            ys = []
            for hh in (h0, h0 + 1):
                seg = acum[:, hh:hh + 1] - acum_t[hh:hh + 1, :]
                dec = jnp.exp(jnp.where(causal, seg, -jnp.inf))
                ys.append(_dot((cb * dec).astype(BF16), xdt_b))
            lane = lax.broadcasted_iota(I32, (Q, V7X_LANES), 1)
            y = jnp.where(lane < head_dim, ys[0], ys[1])
            y = y + y_off[:, lsl] + act_ref[:, sl] * dsk_ref[:, sl]
            zp = z_ref[:, sl]
            yz_ref[:, sl] = y * (zp * _sigmoid(zp))
        xw_g = (xdt_g * exp_ref[DTE, :, gsl]).astype(BF16)
        new_states = _dot(b_f.T.astype(BF16), xw_g)
        state_ref[:, gsl] = s_old * exp_ref[DIN, Q - 1:Q, gsl] + new_states
        yzg = yz_ref[:, g * gw:(g + 1) * gw]
        ms = jnp.mean(yzg * yzg, axis=-1, keepdims=True)
        yn_ref[:, g * gw:(g + 1) * gw] = (
            (yzg * lax.rsqrt(ms + EPS)) * gn_ref[:, g * gw:(g + 1) * gw]).astype(yn_ref.dtype)


def _ssd_scan(xbc, z, dt, cw, cb, a_row, dsk, gn, *, d_inner):
    B, L, cdim = xbc.shape
    Q = SSD_CHUNK
    nd = dt.shape[-1]
    full2 = lambda b, c: (0, 0)
    head_of_col = jnp.arange(d_inner, dtype=I32) // SSD_HEAD_DIM
    e1 = (jnp.arange(nd, dtype=I32)[:, None] == head_of_col[None, :]).astype(BF16)
    e2 = jnp.concatenate([e1, e1], axis=0)
    kern = functools.partial(_ssd_kernel, Q=Q, d_inner=d_inner, n_groups=SSD_N_GROUPS,
                             d_state=SSD_D_STATE, head_dim=SSD_HEAD_DIM)
    rows = SSD_CHUNKS_PER_STEP * Q
    return pl.pallas_call(
        kern,
        grid=(B, L // rows),
        in_specs=[
            pl.BlockSpec((None, rows, cdim), lambda b, c: (b, c, 0)),
            pl.BlockSpec((None, rows, d_inner), lambda b, c: (b, c, 0)),
            pl.BlockSpec((None, rows, nd), lambda b, c: (b, c, 0)),
            pl.BlockSpec(cw.shape, full2),
            pl.BlockSpec(cb.shape, full2),
            pl.BlockSpec(a_row.shape, full2),
            pl.BlockSpec(dsk.shape, full2),
            pl.BlockSpec(gn.shape, full2),
            pl.BlockSpec(e2.shape, full2),
        ],
        out_specs=pl.BlockSpec((None, rows, d_inner), lambda b, c: (b, c, 0)),
        out_shape=jax.ShapeDtypeStruct((B, L, d_inner), BF16),
        scratch_shapes=[
            pltpu.VMEM((SSD_D_STATE, d_inner), F32),
            pltpu.VMEM((Q + V7X_SUBLANES, cdim), F32),
            pltpu.VMEM((Q, cdim), F32),
            pltpu.VMEM((Q, d_inner), F32),
            pltpu.VMEM((3, Q, d_inner), F32),
        ],
        compiler_params=_cparams(2),
        name="ssd_scan",
    )(xbc, z, dt, cw, cb, a_row, dsk, gn, e2)


def _proj_res_kernel(a_ref, w_ref, r_ref, o_ref):
    o_ref[...] = r_ref[...] + _dot(a_ref[...], w_ref[...])


def _proj_res(a, w, res, *, tm):
    T, K = a.shape
    N = w.shape[1]
    return pl.pallas_call(
        _proj_res_kernel,
        grid=(T // tm,),
        in_specs=[
            pl.BlockSpec((tm, K), lambda i: (i, 0)),
            pl.BlockSpec((K, N), lambda i: (0, 0)),
            pl.BlockSpec((tm, N), lambda i: (i, 0)),
        ],
        out_specs=pl.BlockSpec((tm, N), lambda i: (i, 0)),
        out_shape=jax.ShapeDtypeStruct((T, N), F32),
        compiler_params=_cparams(1),
        name="proj_residual",
    )(a, w, res)


def _qkv_kernel(x_ref, g_ref, w_ref, q_ref, k_ref, v_ref, *, n_heads, hd, scale):
    xn = _rmsnorm(x_ref[...], g_ref[...]).astype(BF16)
    qkv = _dot(xn, w_ref[...])
    d = n_heads * hd
    for h in range(n_heads):
        q_ref[h] = (qkv[:, h * hd:(h + 1) * hd] * scale).astype(BF16)
        k_ref[h] = qkv[:, d + h * hd:d + (h + 1) * hd].astype(BF16)
        v_ref[h] = qkv[:, 2 * d + h * hd:2 * d + (h + 1) * hd].astype(BF16)


def _qkv_proj(x, g, w, *, n_heads, hd, tm):
    B, L, D = x.shape
    kern = functools.partial(_qkv_kernel, n_heads=n_heads, hd=hd, scale=1.0 / math.sqrt(hd))
    hspec = pl.BlockSpec((None, n_heads, tm, hd), lambda b, i: (b, 0, i, 0))
    hshape = jax.ShapeDtypeStruct((B, n_heads, L, hd), BF16)
    return pl.pallas_call(
        kern,
        grid=(B, L // tm),
        in_specs=[
            pl.BlockSpec((None, tm, D), lambda b, i: (b, i, 0)),
            pl.BlockSpec((1, D), lambda b, i: (0, 0)),
            pl.BlockSpec(w.shape, lambda b, i: (0, 0)),
        ],
        out_specs=[hspec, hspec, hspec],
        out_shape=[hshape, hshape, hshape],
        compiler_params=_cparams(2),
        name="sb_qkv_proj",
    )(x, g, w)


def _attn_kernel(q_ref, k_ref, v_ref, o_ref, acc_ref, carry_ref, *, tq, tk):
    qi = pl.program_id(1)
    nh, _, hd = q_ref.shape
    n_diag = tq // tk
    rows_t = lax.broadcasted_iota(I32, (tk, tk), 0)
    cols_t = lax.broadcasted_iota(I32, (tk, tk), 1)
    tri = jnp.where(rows_t >= cols_t, 1.0, 0.0).astype(BF16)
    tri2 = jnp.concatenate([tri, tri], axis=0)

    def block(k0, r0, diag):
        nr = tq - r0
        if diag:
            before = (lax.broadcasted_iota(I32, (nr, tk), 1) < lax.broadcasted_iota(I32, (nr, tk), 0))
        zs, sps = [], []
        for h in range(nh):
            z = _dot_nt(q_ref[h, r0:tq, :], k_ref[h, pl.ds(k0, tk), :])
            sp = _softplus(z)
            if diag:
                sp = jnp.where(before, sp, 0.0)
            zs.append(z)
            sps.append(sp)
        hi, lo = _split2(jnp.concatenate(sps, axis=0))
        cs_all = _dot(jnp.concatenate([hi, lo], axis=1), tri2)
        for h in range(nh):
            cs = cs_all[h * nr:(h + 1) * nr, :]
            carry = carry_ref[h, r0:tq, 0:1]
            log_a = zs[h] - cs + carry
            if diag:
                log_a = jnp.where(before, log_a, -jnp.inf)
            a = jnp.exp(log_a)
            acc_ref[h, r0:tq, :] += _dot(a.astype(BF16), v_ref[h, pl.ds(k0, tk), :])
            carry_ref[h, r0:tq, :] = jnp.broadcast_to(carry - cs[:, 0:1], (nr, carry_ref.shape[2]))

    acc_ref[...] = jnp.zeros_like(acc_ref)
    carry_ref[...] = jnp.zeros_like(carry_ref)
    q0 = qi * tq
    for d in range(n_diag - 1, -1, -1):
        block(pl.multiple_of(q0 + d * tk, tk), d * tk, True)

    def cond(st):
        kb, live = st
        return jnp.logical_and(kb >= 0, live > ATTN_DEAD_LOG)

    def body(st):
        kb, _ = st
        block(pl.multiple_of(kb * tk, tk), 0, False)
        return kb - 1, jnp.max(carry_ref[:, :, 0:1])

    lax.while_loop(cond, body, (qi * n_diag - 1, jnp.max(carry_ref[:, :, 0:1])))
    for h in range(nh):
        o_ref[:, h * hd:(h + 1) * hd] = acc_ref[h].astype(o_ref.dtype)


def _attention(q, k, v, *, B, n_heads, tq, tk, nh):
    BH, L, hd = q.shape
    kern = functools.partial(_attn_kernel, tq=tq, tk=tk)
    groups = n_heads // nh
    return pl.pallas_call(
        kern,
        grid=(BH // nh, L // tq),
        in_specs=[
            pl.BlockSpec((nh, tq, hd), lambda g, i: (g, i, 0)),
            pl.BlockSpec((nh, L, hd), lambda g, i: (g, 0, 0), pipeline_mode=pl.Buffered(1)),
            pl.BlockSpec((nh, L, hd), lambda g, i: (g, 0, 0), pipeline_mode=pl.Buffered(1)),
        ],
        out_specs=pl.BlockSpec((None, tq, nh * hd), lambda g, i: (g // groups, i, g % groups)),
        out_shape=jax.ShapeDtypeStruct((B, L, n_heads * hd), BF16),
        scratch_shapes=[pltpu.VMEM((nh, tq, hd), F32), pltpu.VMEM((nh, tq, V7X_LANES), F32)],
        compiler_params=_cparams(2),
        name="sb_attention",
    )(q, k, v)


RI_E0, RI_E1, RI_G0, RI_G1, RI_R0, RI_R1 = 0, 1, 2, 3, 4, 5


def _router_kernel(h_ref, g_ref, whm_ref, b_ref, ri_ref, cnt_ref, carry_ref, *, n_groups, n_experts):
    i = pl.program_id(0)
    tm = h_ref.shape[0]
    epg = n_experts // n_groups

    @pl.when(i == 0)
    def _init():
        carry_ref[...] = jnp.zeros_like(carry_ref)

    xn = _rmsnorm(h_ref[...], g_ref[...])
    xh, xm = _split2(xn)
    both = _dot(xh, whm_ref[...])
    logits = (both[:, :V7X_LANES] + (both[:, V7X_LANES:] + _dot(xm, whm_ref[:, :V7X_LANES]))) + b_ref[...]

    lane = lax.broadcasted_iota(I32, (tm, V7X_LANES), 1).astype(F32)
    neg = -jnp.inf
    no_lane = float(V7X_LANES)
    gl = jnp.where(lane < n_groups, logits, neg)
    gmax = jnp.max(gl, axis=-1, keepdims=True)
    gidx = jnp.min(jnp.where(gl == gmax, lane, no_lane), axis=-1, keepdims=True)
    g_w = 1.0 / jnp.sum(jnp.exp(gl - gmax), axis=-1, keepdims=True)

    first = n_groups + gidx * epg
    in_group = jnp.logical_and(lane >= first, lane < first + epg)
    el = jnp.where(in_group, logits, neg)
    m0 = jnp.max(el, axis=-1, keepdims=True)
    i0 = jnp.min(jnp.where(el == m0, lane, no_lane), axis=-1, keepdims=True)
    el1 = jnp.where(lane == i0, neg, el)
    m1 = jnp.max(el1, axis=-1, keepdims=True)
    i1 = jnp.min(jnp.where(el1 == m1, lane, no_lane), axis=-1, keepdims=True)
    d = jnp.exp(m1 - m0)
    p0 = 1.0 / (1.0 + d)
    gate0 = g_w * p0
    gate1 = g_w * (d * p0)

    oh0 = jnp.where(lane == i0, 1.0, 0.0)
    oh1 = jnp.where(lane == i1, 1.0, 0.0)
    r = lax.broadcasted_iota(I32, (tm, tm), 0)
    cc = lax.broadcasted_iota(I32, (tm, tm), 1)
    strict = jnp.where(cc < r, 1.0, 0.0).astype(BF16)
    pre0 = _dot(strict, oh0.astype(BF16))
    pre1 = _dot(strict, oh1.astype(BF16))
    cnt0 = jnp.sum(oh0, axis=0, keepdims=True)
    cnt1 = jnp.sum(oh1, axis=0, keepdims=True)
    base = carry_ref[0:1, :]
    rank0 = jnp.sum(oh0 * (pre0 + base), axis=-1, keepdims=True)
    rank1 = jnp.sum(oh1 * (pre1 + (base + cnt0)), axis=-1, keepdims=True)
    total = base + cnt0 + cnt1
    carry_ref[...] = jnp.broadcast_to(total, carry_ref.shape)
    cnt_ref[...] = jnp.broadcast_to(total, cnt_ref.shape)

    e0 = i0 - n_groups
    e1 = i1 - n_groups
    ri = jnp.where(lane == RI_E0, e0, 0.0)
    ri = jnp.where(lane == RI_E1, e1, ri)
    ri = jnp.where(lane == RI_G0, gate0, ri)
    ri = jnp.where(lane == RI_G1, gate1, ri)
    ri = jnp.where(lane == RI_R0, rank0, ri)
    ri = jnp.where(lane == RI_R1, rank1, ri)
    ri_ref[...] = ri


def _router(h2, g, w_hm, b_r, *, n_experts, tm):
    T, D = h2.shape
    kern = functools.partial(_router_kernel, n_groups=MOE_GROUPS, n_experts=n_experts)
    return pl.pallas_call(
        kern,
        grid=(T // tm,),
        in_specs=[
            pl.BlockSpec((tm, D), lambda i: (i, 0)),
            pl.BlockSpec((1, D), lambda i: (0, 0)),
            pl.BlockSpec((D, 2 * V7X_LANES), lambda i: (0, 0)),
            pl.BlockSpec((1, V7X_LANES), lambda i: (0, 0)),
        ],
        out_specs=[
            pl.BlockSpec((tm, V7X_LANES), lambda i: (i, 0)),
            pl.BlockSpec((V7X_SUBLANES, V7X_LANES), lambda i: (0, 0)),
        ],
        out_shape=[
            jax.ShapeDtypeStruct((T, V7X_LANES), F32),
            jax.ShapeDtypeStruct((V7X_SUBLANES, V7X_LANES), F32),
        ],
        scratch_shapes=[pltpu.VMEM((V7X_SUBLANES, V7X_LANES), F32)],
        compiler_params=_cparams(1),
        name="moe_router",
    )(h2, g, w_hm, b_r)


def _tile_copy(src_ref, src_row, dst_ref, dst_row, sem, sub=V7X_SUBLANES):
    s = pl.multiple_of(src_row * sub, sub)
    d = pl.multiple_of(dst_row * sub, sub)
    return pltpu.make_async_copy(src_ref.at[pl.ds(s, sub)], dst_ref.at[pl.ds(d, sub)], sem)


def _token_loop(tm, fn):
    def step(t, c):
        fn(t)
        return c

    lax.fori_loop(0, tm, step, 0, unroll=SCALAR_UNROLL)


def _dispatch_kernel(dest_ref, h_ref, g_ref, init_ref, rows_ref, xt_ref, sem):
    del init_ref
    i = pl.program_id(0)
    tm = h_ref.shape[0]
    cur = i % 2
    _packed_rows_store(xt_ref.at[cur], _rmsnorm(h_ref[...], g_ref[...]), tm)

    def issue(t):
        for k in range(MOE_TOP_K):
            _tile_copy(xt_ref.at[cur], t, rows_ref, dest_ref[0, 0, MOE_TOP_K * t + k],
                       sem.at[cur], PACKED_SUBLANES).start(priority=k)

    def drain(b):
        def wait(t):
            for _ in range(MOE_TOP_K):
                _tile_copy(xt_ref.at[b], t, rows_ref, 0, sem.at[b], PACKED_SUBLANES).wait()

        _token_loop(tm, wait)

    _token_loop(tm, issue)

    @pl.when(i >= 1)
    def _prev():
        drain(1 - cur)

    @pl.when(i == pl.num_programs(0) - 1)
    def _last():
        drain(cur)


def _dispatch(dest3, h2, g, n_rows, *, tm):
    T, D = h2.shape
    assert D == 2 * PACKED_SUBLANES * V7X_LANES
    tiles = (n_rows * PACKED_SUBLANES, V7X_LANES)
    init = jnp.zeros(tiles, jnp.uint32)
    return pl.pallas_call(
        _dispatch_kernel,
        grid=(T // tm,),
        in_specs=[
            pl.BlockSpec((1, 1, MOE_TOP_K * tm), lambda i: (i, 0, 0), memory_space=pltpu.SMEM),
            pl.BlockSpec((tm, D), lambda i: (i, 0)),
            pl.BlockSpec((1, D), lambda i: (0, 0)),
            pl.BlockSpec(memory_space=pl.ANY),
        ],
        out_specs=pl.BlockSpec(memory_space=pl.ANY),
        out_shape=jax.ShapeDtypeStruct(tiles, jnp.uint32),
        scratch_shapes=[pltpu.VMEM((2, tm * PACKED_SUBLANES, V7X_LANES), jnp.uint32),
                        pltpu.SemaphoreType.DMA((2,))],
        input_output_aliases={3: 0},
        compiler_params=_cparams(1),
        name="moe_dispatch",
    )(dest3, h2, g, init)


def _expert_kernel(be_ref, nu_ref, x_ref, wg_ref, wu_ref, wd_ref, y_ref):
    i = pl.program_id(0)
    rb = x_ref.shape[0] // PACKED_SUBLANES

    @pl.when(i < nu_ref[0])
    def _live():
        x = _packed_rows_load(x_ref, rb)
        gt = _dot(x, wg_ref[0].astype(BF16))
        up = _dot(x, wu_ref[0].astype(BF16))
        act = (gt * _sigmoid(gt)) * up
        _tile_rows_store(y_ref, _dot(act.astype(BF16), wd_ref[0].astype(BF16)), rb)

    @pl.when(i >= nu_ref[0])
    def _dead():
        y_ref[...] = jnp.zeros_like(y_ref)


def _experts(blk_expert, n_used, x_rows, w_gate, w_up, w_down, *, layer, rb):
    n_blk = x_rows.shape[0] // (rb * PACKED_SUBLANES)
    _, E, D, F = w_gate.shape
    in_blk = (rb * PACKED_SUBLANES, V7X_LANES)
    tile_blk = (rb * V7X_SUBLANES, V7X_LANES)

    def row_map(i, be, nu):
        return (jnp.minimum(i, nu[0] - 1), 0)

    def w_map(i, be, nu):
        return (layer, be[i], 0, 0)

    grid_spec = pltpu.PrefetchScalarGridSpec(
        num_scalar_prefetch=2,
        grid=(n_blk,),
        in_specs=[
            pl.BlockSpec(in_blk, row_map),
            pl.BlockSpec((None, 1, D, F), w_map),
            pl.BlockSpec((None, 1, D, F), w_map),
            pl.BlockSpec((None, 1, F, D), w_map),
        ],
        out_specs=pl.BlockSpec(tile_blk, lambda i, be, nu: (i, 0)),
    )
    return pl.pallas_call(
        _expert_kernel,
        grid_spec=grid_spec,
        out_shape=jax.ShapeDtypeStruct((n_blk * rb * V7X_SUBLANES, V7X_LANES), F32),
        compiler_params=_cparams(1),
        name="moe_experts",
    )(blk_expert, n_used, x_rows, w_gate, w_up, w_down)


def _combine_kernel(dest_ref, destn_ref, h_ref, ri_ref, p_ref, yrows_ref, pn_ref, wg_ref, bg_ref, wp_ref,
                    fn_ref, o_ref, ybuf_ref, sem, *, final_norm):
    i = pl.program_id(0)
    tm = h_ref.shape[0]
    cur = i % 2

    def fetch(d_ref, b):
        def issue(t):
            for k in range(MOE_TOP_K):
                _tile_copy(yrows_ref, d_ref[0, 0, MOE_TOP_K * t + k], ybuf_ref.at[b, k], t,
                           sem.at[b]).start(priority=k)

        _token_loop(tm, issue)

    @pl.when(i == 0)
    def _first():
        fetch(dest_ref, 0)

    @pl.when(i + 1 < pl.num_programs(0))
    def _next():
        fetch(destn_ref, 1 - cur)

    def wait(t):
        for k in range(MOE_TOP_K):
            _tile_copy(yrows_ref, 0, ybuf_ref.at[cur, k], t, sem.at[cur]).wait()

    _token_loop(tm, wait)

    ri = ri_ref[...]
    h = h_ref[...] + (_tile_rows_load(ybuf_ref.at[cur, 0], tm) * ri[:, RI_G0:RI_G0 + 1]
                      + _tile_rows_load(ybuf_ref.at[cur, 1], tm) * ri[:, RI_G1:RI_G1 + 1])
    xn = _rmsnorm(h, pn_ref[...]).astype(BF16)
    gate = _sigmoid(_dot(xn, wg_ref[...]) + bg_ref[...])
    h = h + gate * _dot(p_ref[...].astype(BF16), wp_ref[...])
    if final_norm:
        h = _rmsnorm(h, fn_ref[...])
    o_ref[...] = h


def _combine(dest3, h2, ri, p3, y_rows, pn, wg, bg, wp, fnorm, *, layer, tm, final_norm):
    T, D = h2.shape
    P = p3.shape[2]
    kern = functools.partial(_combine_kernel, final_norm=final_norm)
    row = lambda i: (0, 0)
    last = T // tm - 1
    return pl.pallas_call(
        kern,
        grid=(T // tm,),
        in_specs=[
            pl.BlockSpec((1, 1, MOE_TOP_K * tm), lambda i: (i, 0, 0), memory_space=pltpu.SMEM),
            pl.BlockSpec((1, 1, MOE_TOP_K * tm), lambda i: (jnp.minimum(i + 1, last), 0, 0),
                         memory_space=pltpu.SMEM),
            pl.BlockSpec((tm, D), lambda i: (i, 0)),
            pl.BlockSpec((tm, V7X_LANES), lambda i: (i, 0)),
            pl.BlockSpec((None, tm, P), lambda i: (layer, i, 0)),
            pl.BlockSpec(memory_space=pl.ANY),
            pl.BlockSpec((1, D), row),
            pl.BlockSpec((D, D), row),
            pl.BlockSpec((1, D), row),
            pl.BlockSpec((P, D), row),
            pl.BlockSpec((1, D), row),
        ],
        out_specs=pl.BlockSpec((tm, D), lambda i: (i, 0)),
        out_shape=jax.ShapeDtypeStruct((T, D), F32),
        scratch_shapes=[pltpu.VMEM((2, MOE_TOP_K, tm * V7X_SUBLANES, V7X_LANES), F32),
                        pltpu.SemaphoreType.DMA((2,))],
        compiler_params=_cparams(1),
        name="moe_combine_ple",
    )(dest3, dest3, h2, ri, p3, y_rows, pn, wg, bg, wp, fnorm)


TM_PROJ = 512
TM_RES = 1024
TM_TOKEN = 256
TM_MOVE = 256
EXPERT_ROWS = 512
ATTN_Q_BLOCK = 512
ATTN_K_BLOCK = 256
ATTN_HEADS = 4

def _row(v):
    return v.reshape(1, -1).astype(F32)


def _moe_ple(h2, p3, moe_norm, w_rg, b_rg, w_re, b_re, w_gate, w_up, w_down,
             ple_norm, ple_w_gate, ple_b_gate, ple_w_proj, final_norm, *, layer, last):
    T, D = h2.shape
    G = w_rg.shape[1]
    E = w_re.shape[1]
    TK = T * MOE_TOP_K
    pad = V7X_LANES - G - E
    w_r = jnp.concatenate([w_rg, w_re, jnp.zeros((D, pad), F32)], axis=1)
    b_r = jnp.concatenate([b_rg, b_re, jnp.zeros((pad,), F32)]).reshape(1, V7X_LANES)
    w_hi = w_r.astype(BF16)
    w_mid = (w_r - w_hi.astype(F32)).astype(BF16)
    ri, cnt = _router(h2, _row(moe_norm), jnp.concatenate([w_hi, w_mid], axis=1), b_r,
                      n_experts=E, tm=TM_TOKEN)

    rb = EXPERT_ROWS
    counts = cnt[0, G:G + E].astype(I32)
    padded = ((counts + rb - 1) // rb) * rb
    pad_end = jnp.cumsum(padded)
    pad_start = pad_end - padded
    n_blk = TK // rb + E
    n_rows = n_blk * rb
    eids = jnp.arange(E, dtype=I32)
    expert = ri[:, RI_E0:RI_E1 + 1].astype(I32)
    rank = ri[:, RI_R0:RI_R1 + 1].astype(I32)
    dest = jnp.sum(jnp.where(expert[:, :, None] == eids, pad_start, 0), axis=-1) + rank
    tmv = min(TM_MOVE, T)
    dest3 = dest.reshape(T // tmv, 1, MOE_TOP_K * tmv)
    blk_row0 = jnp.arange(n_blk, dtype=I32) * rb
    blk_expert = jnp.minimum(jnp.sum((pad_end[None, :] <= blk_row0[:, None]).astype(I32), axis=1), E - 1)
    n_used = (pad_end[-1:] // rb).astype(I32)

    x_rows = _dispatch(dest3, h2, _row(moe_norm), n_rows, tm=tmv)
    y_rows = _experts(blk_expert, n_used, x_rows, w_gate, w_up, w_down, layer=layer, rb=rb)
    return _combine(dest3, h2, ri, p3, y_rows, _row(ple_norm), ple_w_gate.astype(BF16), _row(ple_b_gate),
                    ple_w_proj.astype(BF16), _row(final_norm), layer=layer, tm=tmv, final_norm=last)


def _ssd_layer(h2, B, L, norm, w_in, conv_w, conv_b, dt_bias, a_log, d_skip, gnorm, w_out):
    T, D = h2.shape
    H = a_log.shape[0]
    d_inner = H * SSD_HEAD_DIM
    cdim = conv_w.shape[1]
    wz = w_in[:, :d_inner].astype(BF16)
    wx = w_in[:, d_inner:d_inner + cdim].astype(BF16)
    padh = V7X_LANES - H
    wdt = jnp.concatenate([w_in[:, d_inner + cdim:], jnp.zeros((D, padh), F32)], axis=1).astype(BF16)
    dtb = jnp.concatenate([dt_bias, jnp.zeros((padh,), F32)]).reshape(1, V7X_LANES)
    a_row = jnp.concatenate([-jnp.exp(a_log.astype(F32)), jnp.zeros((padh,), F32)]).reshape(1, V7X_LANES)
    dsk = jnp.repeat(d_skip.astype(F32), SSD_HEAD_DIM).reshape(1, d_inner)

    z, xbc, dt = _in_proj(h2, _row(norm), wz, wx, wdt, dtb, tm=TM_PROJ)
    yn = _ssd_scan(xbc.reshape(B, L, cdim), z.reshape(B, L, d_inner), dt.reshape(B, L, V7X_LANES),
                   conv_w.astype(F32), _row(conv_b), a_row, dsk, _row(gnorm), d_inner=d_inner)
    return _proj_res(yn.reshape(T, d_inner), w_out.astype(BF16), h2, tm=TM_RES)


def _sb_layer(h2, B, L, norm, w_qkv, w_o):
    T, D = h2.shape
    hd = SB_HEAD_DIM
    n_heads = w_o.shape[0] // hd
    q, k, v = _qkv_proj(h2.reshape(B, L, D), _row(norm), w_qkv.astype(BF16),
                        n_heads=n_heads, hd=hd, tm=TM_PROJ)
    bh = B * n_heads
    o = _attention(q.reshape(bh, L, hd), k.reshape(bh, L, hd), v.reshape(bh, L, hd),
                   B=B, n_heads=n_heads, tq=min(ATTN_Q_BLOCK, L), tk=ATTN_K_BLOCK, nh=ATTN_HEADS)
    return _proj_res(o.reshape(T, n_heads * hd), w_o.astype(BF16), h2, tm=TM_RES)


def kernel(x, p, ssd_norm, ssd_w_in, ssd_conv_w, ssd_conv_b, ssd_dt_bias, ssd_a_log, ssd_d, ssd_gnorm, ssd_w_out, sb_norm, sb_w_qkv, sb_w_o, moe_norm, moe_w_rg, moe_b_rg, moe_w_re, moe_b_re, moe_w_gate, moe_w_up, moe_w_down, ple_norm, ple_w_gate, ple_b_gate, ple_w_proj, final_norm):
    B, L, D = x.shape
    depth = p.shape[0]
    T = B * L
    n_mixers = 2
    h = x.reshape(T, D)
    p3 = p.reshape(depth, T, p.shape[-1])
    for i in range(depth):
        j = i // n_mixers
        if i % n_mixers == 0:
            h = _ssd_layer(h, B, L, ssd_norm[j], ssd_w_in[j], ssd_conv_w[j], ssd_conv_b[j], ssd_dt_bias[j],
                           ssd_a_log[j], ssd_d[j], ssd_gnorm[j], ssd_w_out[j])
        else:
            h = _sb_layer(h, B, L, sb_norm[j], sb_w_qkv[j], sb_w_o[j])
        h = _moe_ple(h, p3, moe_norm[i], moe_w_rg[i], moe_b_rg[i], moe_w_re[i], moe_b_re[i],
                     moe_w_gate, moe_w_up, moe_w_down, ple_norm[i], ple_w_gate[i], ple_b_gate[i],
                     ple_w_proj[i], final_norm, layer=i, last=(i == depth - 1))
    return h.reshape(B, L, D)
```
